```python
import jax, jax.numpy as jnp
from jax import lax
import numpy as np

D_MODEL = 2048
BATCH = 8
SEQ = 2048
DEPTH = 4

N_BRANCH = 4
BRANCH_W = 512

ML_HEADS = 4
ML_DQK = 64
ML_DV = 128
ML_CHUNK = 128
ML_CONV = 3

ATT_WINDOW = (128, 512, 2048)
ATT_DIL = (1, 4, 16)
ATT_NGROUP = 3
ATT_HEADS = 8
ATT_DH = 64
ATT_W = ATT_NGROUP * ATT_HEADS * ATT_DH

SG_CHUNK = 128
SG_GROUPS = 4
SG_DG = BRANCH_W // SG_GROUPS

POOL_WINDOWS = (2, 4, 8, 16)
POOL_DG = BRANCH_W // len(POOL_WINDOWS)

N_EGROUPS = 4
EXP_PER_GROUP = 8
N_EXPERTS = N_EGROUPS * EXP_PER_GROUP
TOP_K = 2
D_EXPERT = 512
MOE_BLOCK = 128

ALPHA = (2.0 * DEPTH) ** 0.25
BETA = (8.0 * DEPTH) ** -0.25
LN_EPS = 1e-5
NEG = -1e30

IN_SIZES = (ML_HEADS * ML_DQK, ML_HEADS * ML_DQK, ML_HEADS * ML_DV, ML_HEADS * ML_DV, 2 * 2 * ML_HEADS,
            ATT_W, ATT_W, ATT_W, BRANCH_W, BRANCH_W, BRANCH_W)
D_IN = sum(IN_SIZES)
IN_SPLITS = tuple(int(c) for c in np.cumsum(IN_SIZES)[:-1])

kernel_name = "hybrid_bidir_mlstm_dilattn_sgmlp_pool_hmoe"


def _layer_norm(x, g, b):
    xf = x.astype(jnp.float32)
    mu = xf.mean(-1, keepdims=True)
    var = jnp.mean(jnp.square(xf - mu), -1, keepdims=True)
    return ((xf - mu) * lax.rsqrt(var + LN_EPS) * g + b).astype(x.dtype)


def _conv_centred(x, w):
    k = w.shape[0]
    p = k // 2
    s = x.shape[1]
    xp = jnp.pad(x, ((0, 0), (p, p), (0, 0)))
    return sum(xp[:, j:j + s] * w[j] for j in range(k))


def _mlstm_scan(q, k, v, ig, lf):
    B, H, S, _ = q.shape
    nc = S // ML_CHUNK

    def chunks(a):
        a = a.reshape(a.shape[:2] + (nc, ML_CHUNK) + a.shape[3:])
        return jnp.moveaxis(a, 2, 0)

    tril = jnp.tril(jnp.ones((ML_CHUNK, ML_CHUNK), dtype=bool))

    def step(carry, inp):
        C, n, m = carry
        qc, kc, vc, ic, fc = inp
        b = jnp.cumsum(fc, axis=-1)
        dmat = jnp.where(tril, b[..., :, None] - b[..., None, :] + ic[..., None, :], -jnp.inf)
        m_inter = m[..., None] + b
        m_t = jnp.maximum(m_inter, dmat.max(-1))
        w_inter = jnp.exp(m_inter - m_t)
        s = jnp.einsum('bhtd,bhsd->bhts', qc, kc) * jnp.exp(dmat - m_t[..., None])
        num = w_inter[..., None] * jnp.einsum('bhvd,bhtd->bhtv', C, qc) + jnp.einsum('bhts,bhsv->bhtv', s, vc)
        den = w_inter * jnp.einsum('bhd,bhtd->bht', n, qc) + s.sum(-1)
        h = num / jnp.maximum(jnp.abs(den), jnp.exp(-m_t))[..., None]
        g = b[..., -1:] - b + ic
        m_new = jnp.maximum(m + b[..., -1], g.max(-1))
        decay = jnp.exp(m + b[..., -1] - m_new)
        wk = jnp.exp(g - m_new[..., None])
        C = decay[..., None, None] * C + jnp.einsum('bhs,bhsv,bhsd->bhvd', wk, vc, kc)
        n = decay[..., None] * n + jnp.einsum('bhs,bhsd->bhd', wk, kc)
        return (C, n, m_new), h

    init = (jnp.zeros((B, H, ML_DV, ML_DQK), jnp.float32),
            jnp.zeros((B, H, ML_DQK), jnp.float32),
            jnp.zeros((B, H), jnp.float32))
    _, h = lax.scan(step, init, (chunks(q), chunks(k), chunks(v), chunks(ig), chunks(lf)))
    return jnp.moveaxis(h, 0, 2).reshape(B, H, S, ML_DV)


def _mlstm_branch(q, k, v, o, gates, conv_w, gate_b, norm_w):
    B, S, _ = q.shape
    qk = jax.nn.silu(_conv_centred(jnp.concatenate([q, k], axis=-1), conv_w))
    q, k = jnp.split(qk, 2, axis=-1)

    def heads(a, d):
        return a.reshape(B, S, ML_HEADS, d).transpose(0, 2, 1, 3).astype(jnp.float32)

    qh = heads(q, ML_DQK) * (ML_DQK ** -0.5)
    kh, vh = heads(k, ML_DQK), heads(v, ML_DV)
    g = (gates.astype(jnp.float32).reshape(B, S, 2, 2, ML_HEADS) + gate_b).transpose(2, 3, 0, 4, 1)
    h_fwd = _mlstm_scan(qh, kh, vh, g[0, 0], jax.nn.log_sigmoid(g[0, 1]))
    fl = lambda a: jnp.flip(a, axis=2)
    h_bwd = fl(_mlstm_scan(fl(qh), fl(kh), fl(vh), fl(g[1, 0]), fl(jax.nn.log_sigmoid(g[1, 1]))))
    h = h_fwd + h_bwd
    mu = h.mean(-1, keepdims=True)
    var = jnp.mean(jnp.square(h - mu), -1, keepdims=True)
    hn = ((h - mu) * lax.rsqrt(var + LN_EPS)).transpose(0, 2, 1, 3).reshape(B, S, ML_HEADS * ML_DV)
    return hn * norm_w * jax.nn.sigmoid(o.astype(jnp.float32))


def _alibi_slopes():
    n = ATT_NGROUP * ATT_HEADS
    s = 2.0 ** (-8.0 * np.arange(1, n + 1) / n)
    return jnp.asarray(s.reshape(ATT_NGROUP, ATT_HEADS), jnp.float32)


def _banded_attention(q, k, v, slope, w):
    N, L, H, Dh = q.shape
    nb = -(-L // w)
    Lp = nb * w
    qb = jnp.pad(q, ((0, 0), (0, Lp - L), (0, 0), (0, 0))).reshape(N, nb, w, H, Dh)
    padk = ((0, 0), (w, Lp - L + w), (0, 0), (0, 0))
    kp, vp = jnp.pad(k, padk), jnp.pad(v, padk)

    def band(a):
        return jnp.concatenate([a[:, j * w:j * w + Lp].reshape(N, nb, w, H, Dh) for j in range(3)], axis=2)

    kb, vb = band(kp), band(vp)
    qi = jnp.arange(nb)[:, None, None] * w + jnp.arange(w)[None, :, None]
    ki = jnp.arange(nb)[:, None, None] * w - w + jnp.arange(3 * w)[None, None, :]
    rel = ki - qi
    valid = (jnp.abs(rel) <= w) & (ki >= 0) & (ki < L)
    s = jnp.einsum('nbqhd,nbkhd->nbhqk', qb, kb).astype(jnp.float32) * (Dh ** -0.5)
    s = s - slope[None, :, None, None] * jnp.abs(rel).astype(jnp.float32)[:, None]
    s = jnp.where(valid[:, None], s, NEG)
    lse = jax.nn.logsumexp(s, axis=-1)
    p = jnp.exp(s - lse[..., None])
    o = jnp.einsum('nbhqk,nbkhd->nbqhd', p, vb).reshape(N, Lp, H, Dh)[:, :L]
    lse = lse.transpose(0, 1, 3, 2).reshape(N, Lp, H)[:, :L]
    return o, lse


def _dilated_attention(q, k, v):
    B, S = q.shape[:2]
    slopes = _alibi_slopes()
    outs, lses = [], []
    for g in range(ATT_NGROUP):
        dil = ATT_DIL[g]
        neigh = ATT_WINDOW[g] // (2 * dil)
        Ls = S // dil

        def to_sub(a):
            return a.reshape(B, Ls, dil, ATT_HEADS, ATT_DH).transpose(0, 2, 1, 3, 4).reshape(B * dil, Ls, ATT_HEADS, ATT_DH)

        o, lse = _banded_attention(to_sub(q[:, :, g]), to_sub(k[:, :, g]), to_sub(v[:, :, g]), slopes[g] * dil, neigh)
        outs.append(o.reshape(B, dil, Ls, ATT_HEADS, ATT_DH).transpose(0, 2, 1, 3, 4).reshape(B, S, ATT_HEADS, ATT_DH))
        lses.append(lse.reshape(B, dil, Ls, ATT_HEADS).transpose(0, 2, 1, 3).reshape(B, S, ATT_HEADS))
    wgt = jax.nn.softmax(jnp.stack(lses, 0), axis=0)
    out = jnp.einsum('gbsh,gbshd->bshd', wgt, jnp.stack(outs, 0))
    return out.reshape(B, S, ATT_HEADS * ATT_DH)


def _spatial_gating(u, v, ln_g, ln_b, w_s, b_s):
    B, S, _ = u.shape
    u, v = jax.nn.gelu(u), jax.nn.gelu(v)
    v = _layer_norm(v, ln_g, ln_b)
    vc = v.reshape(B, S // SG_CHUNK, SG_CHUNK, SG_GROUPS, SG_DG)
    mixed = jnp.einsum('gts,bcsgd->bctgd', w_s, vc) + b_s.T[None, None, :, :, None]
    return u * mixed.reshape(B, S, BRANCH_W)


def _multiscale_pool(p, w_pool, scale):
    B, S, _ = p.shape
    cs = jnp.pad(jnp.cumsum(p.astype(jnp.float32), axis=1), ((0, 0), (1, 0), (0, 0)))
    t = jnp.arange(S)
    outs = []
    for g, win in enumerate(POOL_WINDOWS):
        sl = slice(g * POOL_DG, (g + 1) * POOL_DG)
        lo = jnp.clip(t - win // 2, 0, S)
        hi = jnp.clip(t + win // 2, 0, S)
        mean = (cs[:, hi, sl] - cs[:, lo, sl]) / (hi - lo).astype(jnp.float32)[None, :, None]
        outs.append(mean - p[:, :, sl])
    d = jnp.stack(outs, axis=2)
    y = jnp.einsum('bsgc,gcd->bsgd', d, w_pool).reshape(B, S, BRANCH_W)
    return y * scale


def _mixer(x, w_in, ml_conv_w, ml_gate_b, ml_norm_w, sg_ln_g, sg_ln_b, sg_w, sg_b,
           pool_w, pool_scale, w_gate, b_gate, w_branch, w_out):
    B, S, D = x.shape
    z = jnp.einsum('bsd,dc->bsc', x, w_in)
    mq, mk, mv, mo, mg, aq, ak, av, su, sv, pp = jnp.split(z, IN_SPLITS, axis=-1)
    y_ml = _mlstm_branch(mq, mk, mv, mo, mg, ml_conv_w, ml_gate_b, ml_norm_w)
    heads = lambda a: a.reshape(B, S, ATT_NGROUP, ATT_HEADS, ATT_DH)
    y_at = _dilated_attention(heads(aq), heads(ak), heads(av))
    y_sg = _spatial_gating(su, sv, sg_ln_g, sg_ln_b, sg_w, sg_b)
    y_pl = _multiscale_pool(pp, pool_w, pool_scale)
    ys = jnp.stack([y_ml.astype(x.dtype), y_at.astype(x.dtype), y_sg.astype(x.dtype), y_pl.astype(x.dtype)], axis=2)
    proj = jnp.einsum('bsnc,ncd->bsnd', ys, w_branch)
    gates = jax.nn.sigmoid(jnp.einsum('bsd,de->bse', x, w_gate) + b_gate).reshape(B, S, N_BRANCH, D)
    merged = jnp.einsum('bsnd,bsnd->bsd', gates, proj)
    return jnp.einsum('bsd,de->bse', merged, w_out)


def _expert_blocks(xf, w1, w3, w2, e_id, wts):
    T, D = xf.shape
    A = T * TOP_K
    e_flat = e_id.reshape(A).astype(jnp.int32)
    tok = jnp.arange(A, dtype=jnp.int32) // TOP_K
    counts = jnp.bincount(e_flat, length=N_EXPERTS)
    padded = (counts + MOE_BLOCK - 1) // MOE_BLOCK * MOE_BLOCK
    pad_end = jnp.cumsum(padded)
    pad_start = pad_end - padded
    raw_start = jnp.cumsum(counts) - counts
    order = jnp.argsort(e_flat)
    e_sorted = e_flat[order]
    dest = pad_start[e_sorted] + jnp.arange(A) - raw_start[e_sorted]
    nb = -(-A // MOE_BLOCK) + N_EXPERTS
    slot_tok = jnp.zeros(nb * MOE_BLOCK, jnp.int32).at[dest].set(tok[order])
    slot_w = jnp.zeros(nb * MOE_BLOCK, wts.dtype).at[dest].set(wts.reshape(A)[order])
    blk_e = jnp.minimum(jnp.searchsorted(pad_end, jnp.arange(nb) * MOE_BLOCK, side='right'), N_EXPERTS - 1)

    def run(args):
        e, idx, w = args
        xb = xf[idx]
        h = jax.nn.silu(xb @ w1[e]) * (xb @ w3[e])
        return (h @ w2[e]) * w[:, None].astype(xf.dtype)

    yb = lax.map(run, (blk_e, slot_tok.reshape(nb, MOE_BLOCK), slot_w.reshape(nb, MOE_BLOCK)))
    return jax.ops.segment_sum(yb.reshape(nb * MOE_BLOCK, D), slot_tok, num_segments=T)


def _hmoe(x, w_rg, b_rg, w_re, b_re, w1, w3, w2):
    B, S, D = x.shape
    T = B * S
    xf = x.reshape(T, D)
    lg = (xf @ w_rg).astype(jnp.float32) + b_rg
    pg = jax.nn.softmax(lg, axis=-1)
    g_sel = jnp.argmax(lg, axis=-1)
    p_sel = jnp.take_along_axis(pg, g_sel[:, None], axis=-1)
    le = ((xf @ w_re).astype(jnp.float32) + b_re).reshape(T, N_EGROUPS, EXP_PER_GROUP)
    le = jnp.take_along_axis(le, g_sel[:, None, None], axis=1)[:, 0]
    top_l, top_i = lax.top_k(le, TOP_K)
    wts = jax.nn.softmax(top_l, axis=-1) * p_sel
    e_id = g_sel[:, None] * EXP_PER_GROUP + top_i
    return _expert_blocks(xf, w1, w3, w2, e_id, wts).reshape(B, S, D)


def setup_inputs(seed: int = 0) -> dict:
    key = jax.random.key(seed)
    ks = jax.random.split(key, 32)
    L, D = DEPTH, D_MODEL
    nrm = lambda i, shape, scale: jax.random.normal(ks[i], shape, jnp.float32) * scale
    x = nrm(0, (BATCH, SEQ, D), 1.0)
    w_in = nrm(1, (L, D, D_IN), D ** -0.5)
    ml_conv_w = 1.0 / ML_CONV + nrm(2, (L, ML_CONV, 2 * ML_HEADS * ML_DQK), 0.2)
    ig_b = nrm(3, (L, 2, 1, ML_HEADS), 0.1)
    fg_b = 3.0 + 3.0 * jax.random.uniform(ks[4], (L, 2, 1, ML_HEADS), jnp.float32)
    ml_gate_b = jnp.concatenate([ig_b, fg_b], axis=2)
    ml_norm_w = 1.0 + nrm(5, (L, ML_HEADS * ML_DV), 0.02)
    sg_ln_g = 1.0 + nrm(6, (L, BRANCH_W), 0.02)
    sg_ln_b = nrm(7, (L, BRANCH_W), 0.02)
    sg_w = nrm(8, (L, SG_GROUPS, SG_CHUNK, SG_CHUNK), SG_CHUNK ** -0.5)
    sg_b = 1.0 + nrm(9, (L, SG_GROUPS, SG_CHUNK), 0.02)
    pool_w = nrm(10, (L, len(POOL_WINDOWS), POOL_DG, POOL_DG), POOL_DG ** -0.5)
    pool_scale = 1.0 + nrm(11, (L, BRANCH_W), 0.02)
    w_gate = nrm(12, (L, D, N_BRANCH * D), D ** -0.5)
    b_gate = nrm(13, (L, N_BRANCH * D), 0.02)
    w_branch = nrm(14, (L, N_BRANCH, BRANCH_W, D), BRANCH_W ** -0.5 * BETA)
    w_out = nrm(15, (L, D, D), D ** -0.5 * BETA)
    ln1_g = 1.0 + nrm(16, (L, D), 0.02)
    ln1_b = nrm(17, (L, D), 0.02)
    w_router_group = nrm(18, (L, D, N_EGROUPS), D ** -0.5)
    b_router_group = nrm(19, (L, N_EGROUPS), 0.01)
    w_router_expert = nrm(20, (L, D, N_EXPERTS), D ** -0.5)
    b_router_expert = nrm(21, (L, N_EXPERTS), 0.01)
    w_exp_gate = nrm(22, (L, N_EXPERTS, D, D_EXPERT), D ** -0.5)
    w_exp_up = nrm(23, (L, N_EXPERTS, D, D_EXPERT), D ** -0.5)
    w_exp_down = nrm(24, (L, N_EXPERTS, D_EXPERT, D), D_EXPERT ** -0.5 * BETA)
    ln2_g = 1.0 + nrm(25, (L, D), 0.02)
    ln2_b = nrm(26, (L, D), 0.02)
    return {"x": x, "w_in": w_in, "ml_conv_w": ml_conv_w, "ml_gate_b": ml_gate_b, "ml_norm_w": ml_norm_w,
            "sg_ln_g": sg_ln_g, "sg_ln_b": sg_ln_b, "sg_w": sg_w, "sg_b": sg_b,
            "pool_w": pool_w, "pool_scale": pool_scale, "w_gate": w_gate, "b_gate": b_gate,
            "w_branch": w_branch, "w_out": w_out, "ln1_g": ln1_g, "ln1_b": ln1_b,
            "w_router_group": w_router_group, "b_router_group": b_router_group,
            "w_router_expert": w_router_expert, "b_router_expert": b_router_expert,
            "w_exp_gate": w_exp_gate, "w_exp_up": w_exp_up, "w_exp_down": w_exp_down,
            "ln2_g": ln2_g, "ln2_b": ln2_b}


def reference(x, w_in, ml_conv_w, ml_gate_b, ml_norm_w, sg_ln_g, sg_ln_b, sg_w, sg_b,
              pool_w, pool_scale, w_gate, b_gate, w_branch, w_out, ln1_g, ln1_b,
              w_router_group, b_router_group, w_router_expert, b_router_expert,
              w_exp_gate, w_exp_up, w_exp_down, ln2_g, ln2_b):
    for l in range(DEPTH):
        y = _mixer(x, w_in[l], ml_conv_w[l], ml_gate_b[l], ml_norm_w[l], sg_ln_g[l], sg_ln_b[l],
                   sg_w[l], sg_b[l], pool_w[l], pool_scale[l], w_gate[l], b_gate[l], w_branch[l], w_out[l])
        x = _layer_norm(ALPHA * x + y, ln1_g[l], ln1_b[l])
        y = _hmoe(x, w_router_group[l], b_router_group[l], w_router_expert[l], b_router_expert[l],
                  w_exp_gate[l], w_exp_up[l], w_exp_down[l])
        x = _layer_norm(ALPHA * x + y, ln2_g[l], ln2_b[l])
    return x
```

```python
import functools

import numpy as np
import jax
import jax.numpy as jnp
from jax import lax
from jax.experimental import pallas as pl
from jax.experimental.pallas import tpu as pltpu

D_MODEL = 2048
DEPTH = 4
BRANCH_W = 512
N_BRANCH = 4

ML_HEADS = 4
ML_DQK = 64
ML_DV = 128
ML_CHUNK = 128
ML_QK_W = ML_HEADS * ML_DQK
ML_V_W = ML_HEADS * ML_DV
ML_GATES = 2 * 2 * ML_HEADS
ML_W = 2 * ML_QK_W + 2 * ML_V_W

ATT_WINDOW = (128, 512, 2048)
ATT_DIL = (1, 4, 16)
ATT_NGROUP = 3
ATT_HEADS = 8
ATT_DH = 64
ATT_GW = ATT_HEADS * ATT_DH
ATT_W = ATT_NGROUP * ATT_GW
ATT_NEIGH = 64
ATT_QT = 128

SG_CHUNK = 128
SG_GROUPS = 4
POOL_WINDOWS = (2, 4, 8, 16)
POOL_DG = BRANCH_W // len(POOL_WINDOWS)

N_EGROUPS = 4
EXP_PER_GROUP = 8
N_EXPERTS = N_EGROUPS * EXP_PER_GROUP
TOP_K = 2
D_EXPERT = 512
ROUTER_EOFF = 32

ALPHA = (2.0 * DEPTH) ** 0.25
LN_EPS = 1e-5
NEG = -1e30

LANES = 128
VMEM_BYTES_V7X = 64 * 1024 * 1024


def _cparams(semantics, vmem_mb):
    assert vmem_mb * 1024 * 1024 < VMEM_BYTES_V7X
    return pltpu.CompilerParams(dimension_semantics=semantics, vmem_limit_bytes=vmem_mb * 1024 * 1024)


def _sigmoid(x):
    return 1.0 / (1.0 + jnp.exp(-x))


def _bdot(a, b):
    return jnp.dot(a, b, preferred_element_type=jnp.float32)


def _layer_norm(x, g, b):
    mu = jnp.mean(x, axis=-1, keepdims=True)
    xc = x - mu
    var = jnp.mean(xc * xc, axis=-1, keepdims=True)
    return xc * lax.rsqrt(var + LN_EPS) * g + b


def _mm_kernel(x_ref, w_ref, o_ref):
    o_ref[...] = _bdot(x_ref[...], w_ref[...]).astype(o_ref.dtype)


def _matmul(x, w, layer, out_dtype, tm, tn):
    t, k = x.shape
    n = w.shape[2]
    return pl.pallas_call(
        _mm_kernel,
        grid=(t // tm, n // tn),
        in_specs=[pl.BlockSpec((tm, k), lambda i, j: (i, 0)),
                  pl.BlockSpec((None, k, tn), lambda i, j: (layer, 0, j))],
        out_specs=pl.BlockSpec((tm, tn), lambda i, j: (i, j)),
        out_shape=jax.ShapeDtypeStruct((t, n), out_dtype),
        compiler_params=_cparams(("parallel", "arbitrary"), 40),
    )(x, w)


def _split_hi_lo(x):
    hi = x.astype(jnp.bfloat16)
    lo = (x - hi.astype(jnp.float32)).astype(jnp.bfloat16)
    return hi, lo


def _mlstm_kernel(q_ref, k_ref, v_ref, o_ref, g_ref, cw_ref, gb_ref, nw_ref, y_ref,
                  qc_s, kc_s, gp_s, hf_s, hb_s, ct_s):
    s_len = q_ref.shape[0]
    n_chunks = s_len // ML_CHUNK
    lc = ML_CHUNK

    row_c = lax.broadcasted_iota(jnp.int32, (lc, 1), 0)
    cw = cw_ref[...]
    gate_lane = lax.broadcasted_iota(jnp.int32, (lc, LANES), 1)

    def prep(c, carry):
        r0 = pl.multiple_of(c * lc, lc)
        rows = pl.ds(r0, lc)

        def conv_silu(ref, w):
            x = ref[rows, :]
            prev_row = jnp.where(c > 0, ref[pl.ds(jnp.maximum(r0 - 1, 0), 1), :], 0.0)
            next_row = jnp.where(c < n_chunks - 1, ref[pl.ds(jnp.minimum(r0 + lc, s_len - 1), 1), :], 0.0)
            x_prev = jnp.where(row_c == 0, prev_row, pltpu.roll(x, 1, 0))
            x_next = jnp.where(row_c == lc - 1, next_row, pltpu.roll(x, lc - 1, 0))
            y = x_prev * w[0:1] + x * w[1:2] + x_next * w[2:3]
            return y * _sigmoid(y)

        qc_s[rows, :] = (conv_silu(q_ref, cw[:, :ML_QK_W]) * (ML_DQK ** -0.5)).astype(jnp.bfloat16)
        kc_s[rows, :] = conv_silu(k_ref, cw[:, ML_QK_W:]).astype(jnp.bfloat16)

        gp = g_ref[rows, :] + gb_ref[...]
        log_sig = jnp.minimum(gp, 0.0) - jnp.log(1.0 + jnp.exp(-jnp.abs(gp)))
        gp_s[rows, :] = jnp.where((gate_lane % 8) >= ML_HEADS, log_sig, gp)
        return carry

    lax.fori_loop(0, n_chunks, prep, 0)

    r_i = lax.broadcasted_iota(jnp.int32, (lc, lc), 0)
    c_i = lax.broadcasted_iota(jnp.int32, (lc, lc), 1)
    lower = c_i <= r_i
    upper = c_i >= r_i
    tril = jnp.where(lower, 1.0, 0.0).astype(jnp.bfloat16)
    triu = jnp.where(upper, 1.0, 0.0).astype(jnp.bfloat16)
    ones_ext = jnp.ones((lc, ML_DV), jnp.bfloat16)
    head_of_lane = lax.broadcasted_iota(jnp.int32, (lc, ML_QK_W), 1) // ML_DQK

    ct_s[...] = jnp.zeros(ct_s.shape, ct_s.dtype)

    def chunk_dir(c, direction, ms, h_out):
        r0 = pl.multiple_of(c * lc, lc)
        rows = pl.ds(r0, lc)
        gc = gp_s[rows, :]
        gct = gc.T
        gc_hi, gc_lo = _split_hi_lo(gc)
        gct_hi, gct_lo = _split_hi_lo(gct)
        if direction == 0:
            cum_col = _bdot(tril, gc_hi) + _bdot(tril, gc_lo)
            cum_row = _bdot(gct_hi, triu) + _bdot(gct_lo, triu)
            mask = lower
        else:
            cum_col = _bdot(triu, gc_hi) + _bdot(triu, gc_lo)
            cum_row = _bdot(gct_hi, tril) + _bdot(gct_lo, tril)
            mask = upper
        qch = qc_s[rows, :]
        kch = kc_s[rows, :]
        vch = v_ref[rows, :].astype(jnp.bfloat16)
        kf = kch.astype(jnp.float32)

        new_ms = []
        wk_cols = []
        decays = []
        for h in range(ML_HEADS):
            ci = direction * 8 + h
            cf = ci + ML_HEADS
            m = ms[h]
            b_col = cum_col[:, cf:cf + 1]
            b_row = cum_row[cf:cf + 1, :]
            i_row = gct[ci:ci + 1, :]
            i_col = gc[:, ci:ci + 1]
            total = b_row[:, lc - 1:lc] if direction == 0 else b_row[:, 0:1]
            dmat = jnp.where(mask, b_col - b_row + i_row, NEG)
            m_inter = m + b_col
            m_t = jnp.maximum(m_inter, jnp.max(dmat, axis=1, keepdims=True))
            w_inter = jnp.exp(m_inter - m_t)
            qh = qch[:, h * ML_DQK:(h + 1) * ML_DQK]
            kh = kch[:, h * ML_DQK:(h + 1) * ML_DQK]
            v_ext = jnp.concatenate([vch[:, h * ML_DV:(h + 1) * ML_DV], ones_ext], axis=1)
            s = lax.dot_general(qh, kh, (((1,), (1,)), ((), ())),
                                preferred_element_type=jnp.float32) * jnp.exp(dmat - m_t)
            ct = ct_s[direction * ML_HEADS + h]
            both = w_inter * _bdot(qh, ct.astype(jnp.bfloat16)) + _bdot(s.astype(jnp.bfloat16), v_ext)
            num = both[:, :ML_DV]
            den = both[:, ML_DV:]
            h_out[rows, h * ML_DV:(h + 1) * ML_DV] = num / jnp.maximum(jnp.abs(den), jnp.exp(-m_t))

            g_row = total - b_row + i_row
            m_new = jnp.maximum(m + total, jnp.max(g_row, axis=1, keepdims=True))
            decays.append(jnp.exp(m + total - m_new))
            wk_cols.append(jnp.exp(total - b_col + i_col - m_new))
            new_ms.append(m_new)

        wk_all = wk_cols[0]
        for h in range(1, ML_HEADS):
            wk_all = jnp.where(head_of_lane >= h, wk_cols[h], wk_all)
        kwt = (kf * wk_all).T.astype(jnp.bfloat16)
        for h in range(ML_HEADS):
            v_ext = jnp.concatenate([vch[:, h * ML_DV:(h + 1) * ML_DV], ones_ext], axis=1)
            idx = direction * ML_HEADS + h
            ct_s[idx] = decays[h] * ct_s[idx] + _bdot(kwt[h * ML_DQK:(h + 1) * ML_DQK, :], v_ext)
        return new_ms

    def body(c, carry):
        ms_f = chunk_dir(c, 0, list(carry[:ML_HEADS]), hf_s)
        ms_b = chunk_dir(n_chunks - 1 - c, 1, list(carry[ML_HEADS:]), hb_s)
        return tuple(ms_f) + tuple(ms_b)

    zero = jnp.zeros((1, 1), jnp.float32)
    lax.fori_loop(0, n_chunks, body, (zero,) * (2 * ML_HEADS))

    nw = nw_ref[...]

    def finish(c, carry):
        rows = pl.ds(pl.multiple_of(c * lc, lc), lc)
        hsum = hf_s[rows, :] + hb_s[rows, :]
        og = _sigmoid(o_ref[rows, :])
        for h in range(ML_HEADS):
            sl = slice(h * ML_DV, (h + 1) * ML_DV)
            hh = hsum[:, sl]
            mu = jnp.mean(hh, axis=-1, keepdims=True)
            hc = hh - mu
            var = jnp.mean(hc * hc, axis=-1, keepdims=True)
            y_ref[rows, sl] = (hc * lax.rsqrt(var + LN_EPS) * nw[:, sl] * og[:, sl]).astype(y_ref.dtype)
        return carry

    lax.fori_loop(0, n_chunks, finish, 0)


def _mlstm(z_ml, z_g, conv_w, gate_b, norm_w, layer, batch, seq):
    return pl.pallas_call(
        _mlstm_kernel,
        grid=(batch,),
        in_specs=[pl.BlockSpec((None, seq, ML_QK_W), lambda b: (b, 0, 0)),
                  pl.BlockSpec((None, seq, ML_QK_W), lambda b: (b, 0, 1)),
                  pl.BlockSpec((None, seq, ML_V_W), lambda b: (b, 0, 1)),
                  pl.BlockSpec((None, seq, ML_V_W), lambda b: (b, 0, 2)),
                  pl.BlockSpec((None, seq, LANES), lambda b: (b, 0, 0)),
                  pl.BlockSpec((None, 3, 2 * ML_QK_W), lambda b: (layer, 0, 0)),
                  pl.BlockSpec((None, 1, LANES), lambda b: (layer, 0, 0)),
                  pl.BlockSpec((None, 1, ML_V_W), lambda b: (layer, 0, 0))],
        out_specs=pl.BlockSpec((None, seq, ML_V_W), lambda b: (b, 0, 0)),
        out_shape=jax.ShapeDtypeStruct((batch, seq, ML_V_W), jnp.bfloat16),
        scratch_shapes=[pltpu.VMEM((seq, ML_QK_W), jnp.bfloat16),
                        pltpu.VMEM((seq, ML_QK_W), jnp.bfloat16),
                        pltpu.VMEM((seq, LANES), jnp.float32),
                        pltpu.VMEM((seq, ML_V_W), jnp.float32),
                        pltpu.VMEM((seq, ML_V_W), jnp.float32),
                        pltpu.VMEM((2 * ML_HEADS, ML_DQK, 2 * ML_DV), jnp.float32)],
        compiler_params=_cparams(("parallel",), 56),
    )(z_ml, z_ml, z_ml, z_ml, z_g, conv_w, gate_b, norm_w)


def _alibi_slopes():
    n = ATT_NGROUP * ATT_HEADS
    s = 2.0 ** (-8.0 * np.arange(1, n + 1) / n)
    return s.reshape(ATT_NGROUP, ATT_HEADS).astype(np.float32)


def _attn_kernel(q_ref, k_ref, v_ref, o_ref, l_ref, *, sub_len, slopes):
    kw = min(2 * ATT_QT, sub_len)
    n_tiles = sub_len // ATT_QT

    def tile(ti, carry):
        if n_tiles == 1:
            q0, ks = 0, 0
        else:
            q0 = pl.multiple_of(ti * ATT_QT, ATT_QT)
            ks = pl.multiple_of(jnp.clip(q0 - ATT_NEIGH, 0, sub_len - kw), ATT_NEIGH)
        qpos = q0 + lax.broadcasted_iota(jnp.int32, (ATT_QT, 1), 0)
        kpos = ks + lax.broadcasted_iota(jnp.int32, (1, kw), 1)
        arel = jnp.abs(kpos - qpos).astype(jnp.float32)
        valid = arel <= float(ATT_NEIGH)
        q = q_ref[pl.ds(q0, ATT_QT), :]
        k = k_ref[pl.ds(ks, kw), :]
        v = v_ref[pl.ds(ks, kw), :]
        o_parts, l_parts = [], []
        for h in range(ATT_HEADS):
            sl = slice(h * ATT_DH, (h + 1) * ATT_DH)
            s = lax.dot_general(q[:, sl], k[:, sl], (((1,), (1,)), ((), ())),
                                preferred_element_type=jnp.float32) * (ATT_DH ** -0.5)
            s = jnp.where(valid, s - float(slopes[h]) * arel, NEG)
            m = jnp.max(s, axis=1, keepdims=True)
            p = jnp.exp(s - m)
            den = jnp.sum(p, axis=1, keepdims=True)
            o_parts.append(_bdot(p.astype(jnp.bfloat16), v[:, sl]) / den)
            l_parts.append(jnp.broadcast_to(m + jnp.log(den), (ATT_QT, ATT_DH)))
        o_ref[pl.ds(q0, ATT_QT), :] = jnp.concatenate(o_parts, axis=1)
        l_ref[pl.ds(q0, ATT_QT), :] = jnp.concatenate(l_parts, axis=1)
        return carry

    if n_tiles == 1:
        tile(0, 0)
    else:
        lax.fori_loop(0, n_tiles, tile, 0)


def _attention_group(z_at, group, batch, seq):
    dil = ATT_DIL[group]
    assert ATT_WINDOW[group] // (2 * dil) == ATT_NEIGH
    sub_len = seq // dil
    blocks_per_row = 3 * ATT_NGROUP
    z_view = z_at.reshape(batch, sub_len, dil * 3 * ATT_W)
    slopes = _alibi_slopes()[group] * np.float32(dil)

    def in_spec(which):
        return pl.BlockSpec((None, sub_len, ATT_GW),
                            lambda b, r: (b, 0, r * blocks_per_row + which * ATT_NGROUP + group))

    out_spec = pl.BlockSpec((None, sub_len, ATT_GW), lambda b, r: (b, 0, r))
    out_sds = jax.ShapeDtypeStruct((batch, sub_len, dil * ATT_GW), jnp.float32)
    o, lse = pl.pallas_call(
        functools.partial(_attn_kernel, sub_len=sub_len, slopes=tuple(float(s) for s in slopes)),
        grid=(batch, dil),
        in_specs=[in_spec(0), in_spec(1), in_spec(2)],
        out_specs=[out_spec, out_spec],
        out_shape=[out_sds, out_sds],
        compiler_params=_cparams(("parallel", "parallel"), 48),
    )(z_view, z_view, z_view)
    return o.reshape(batch * seq, ATT_GW), lse.reshape(batch * seq, ATT_GW)


def _attn_combine_kernel(o0, o1, o2, l0, l1, l2, y_ref):
    la, lb, lc = l0[...], l1[...], l2[...]
    m = jnp.maximum(jnp.maximum(la, lb), lc)
    ea, eb, ec = jnp.exp(la - m), jnp.exp(lb - m), jnp.exp(lc - m)
    y = (ea * o0[...] + eb * o1[...] + ec * o2[...]) / (ea + eb + ec)
    y_ref[...] = y.astype(y_ref.dtype)


def _attn_combine(outs, lses, tm):
    t = outs[0].shape[0]
    spec = pl.BlockSpec((tm, ATT_GW), lambda i: (i, 0))
    return pl.pallas_call(
        _attn_combine_kernel,
        grid=(t // tm,),
        in_specs=[spec] * 6,
        out_specs=spec,
        out_shape=jax.ShapeDtypeStruct((t, ATT_GW), jnp.bfloat16),
        compiler_params=_cparams(("parallel",), 48),
    )(*outs, *lses)


def _gelu_tanh(x):
    return x * (0.5 * (1.0 + jnp.tanh(np.sqrt(2.0 / np.pi).astype(np.float32) * (x + 0.044715 * (x * x * x)))))


def _sg_kernel(u_ref, v_ref, g_ref, b_ref, ws_ref, bs_ref, y_ref):
    tm = u_ref.shape[0]
    dg = BRANCH_W // SG_GROUPS
    u = _gelu_tanh(u_ref[...])
    vn = _layer_norm(_gelu_tanh(v_ref[...]), g_ref[...], b_ref[...]).astype(jnp.bfloat16)
    bs = bs_ref[...]
    for c in range(tm // SG_CHUNK):
        rows = slice(c * SG_CHUNK, (c + 1) * SG_CHUNK)
        for g in range(SG_GROUPS):
            cols = slice(g * dg, (g + 1) * dg)
            mixed = _bdot(ws_ref[g], vn[rows, cols]) + bs[:, cols]
            y_ref[rows, cols] = (u[rows, cols] * mixed).astype(y_ref.dtype)


def _spatial_gating(z_sp, ln_g, ln_b, w_s, b_s_exp, layer, tm):
    t = z_sp.shape[0]
    return pl.pallas_call(
        _sg_kernel,
        grid=(t // tm,),
        in_specs=[pl.BlockSpec((tm, BRANCH_W), lambda i: (i, 0)),
                  pl.BlockSpec((tm, BRANCH_W), lambda i: (i, 1)),
                  pl.BlockSpec((None, 1, BRANCH_W), lambda i: (layer, 0, 0)),
                  pl.BlockSpec((None, 1, BRANCH_W), lambda i: (layer, 0, 0)),
                  pl.BlockSpec((None, SG_GROUPS, SG_CHUNK, SG_CHUNK), lambda i: (layer, 0, 0, 0)),
                  pl.BlockSpec((None, SG_CHUNK, BRANCH_W), lambda i: (layer, 0, 0))],
        out_specs=pl.BlockSpec((tm, BRANCH_W), lambda i: (i, 0)),
        out_shape=jax.ShapeDtypeStruct((t, BRANCH_W), jnp.bfloat16),
        compiler_params=_cparams(("parallel",), 32),
    )(z_sp, z_sp, ln_g, ln_b, w_s, b_s_exp)


def _pool_kernel(p_ref, w_ref, sc_ref, y_ref):
    s_len = p_ref.shape[0]
    row = lax.broadcasted_iota(jnp.int32, (s_len, 1), 0)
    sc = sc_ref[...]
    for g, win in enumerate(POOL_WINDOWS):
        cols = slice(g * POOL_DG, (g + 1) * POOL_DG)
        half = win // 2
        p = p_ref[:, cols]
        acc = jnp.zeros_like(p)
        for j in range(-half, half):
            if j == 0:
                acc = acc + p
            else:
                shifted = pltpu.roll(p, (-j) % s_len, 0)
                ok = (row + j >= 0) & (row + j < s_len)
                acc = acc + jnp.where(ok, shifted, 0.0)
        cnt = (jnp.minimum(row + half, s_len) - jnp.maximum(row - half, 0)).astype(jnp.float32)
        d = acc / cnt - p
        y = _bdot(d.astype(jnp.bfloat16), w_ref[g]) * sc[:, cols]
        y_ref[:, cols] = y.astype(y_ref.dtype)


def _multiscale_pool(z_sp, pool_w, pool_scale, layer, batch, seq):
    return pl.pallas_call(
        _pool_kernel,
        grid=(batch,),
        in_specs=[pl.BlockSpec((None, seq, BRANCH_W), lambda b: (b, 0, 2)),
                  pl.BlockSpec((None, len(POOL_WINDOWS), POOL_DG, POOL_DG), lambda b: (layer, 0, 0, 0)),
                  pl.BlockSpec((None, 1, BRANCH_W), lambda b: (layer, 0, 0))],
        out_specs=pl.BlockSpec((None, seq, BRANCH_W), lambda b: (b, 0, 0)),
        out_shape=jax.ShapeDtypeStruct((batch, seq, BRANCH_W), jnp.bfloat16),
        compiler_params=_cparams(("parallel",), 48),
    )(z_sp, pool_w, pool_scale)


def _merge_kernel(xb_ref, y0, y1, y2, y3, wg0, wg1, wg2, wg3, bg0, bg1, bg2, bg3, wb_ref, wo_ref,
                  xf_ref, g_ref, b_ref, of_ref, ob_ref, acc_ref):
    j = pl.program_id(1)
    xb = xb_ref[...]
    merged = None
    for n, (y, wg, bg) in enumerate(((y0, wg0, bg0), (y1, wg1, bg1), (y2, wg2, bg2), (y3, wg3, bg3))):
        gate = _sigmoid(_bdot(xb, wg[...]) + bg[...])
        term = gate * _bdot(y[...], wb_ref[n])
        merged = term if merged is None else merged + term
    part = _bdot(merged.astype(jnp.bfloat16), wo_ref[...])

    @pl.when(j == 0)
    def _():
        acc_ref[...] = part

    @pl.when(j > 0)
    def _():
        acc_ref[...] += part

    @pl.when(j == pl.num_programs(1) - 1)
    def _():
        out = _layer_norm(ALPHA * xf_ref[...] + acc_ref[...], g_ref[...], b_ref[...])
        of_ref[...] = out
        ob_ref[...] = out.astype(ob_ref.dtype)


def _merge(xb, xf, ys, w_gate, b_gate, w_branch, w_out, ln_g, ln_b, layer, tm, tn):
    t = xb.shape[0]
    nj = D_MODEL // tn
    row_spec = lambda w: pl.BlockSpec((tm, w), lambda i, j: (i, 0))
    wg_specs = [pl.BlockSpec((None, D_MODEL, tn), lambda i, j, n=n: (layer, 0, n * nj + j)) for n in range(N_BRANCH)]
    bg_specs = [pl.BlockSpec((None, 1, tn), lambda i, j, n=n: (layer, 0, n * nj + j)) for n in range(N_BRANCH)]
    vec_spec = pl.BlockSpec((None, 1, D_MODEL), lambda i, j: (layer, 0, 0))
    return pl.pallas_call(
        _merge_kernel,
        grid=(t // tm, nj),
        in_specs=[row_spec(D_MODEL)] + [row_spec(BRANCH_W)] * N_BRANCH + wg_specs + bg_specs
                 + [pl.BlockSpec((None, N_BRANCH, BRANCH_W, tn), lambda i, j: (layer, 0, 0, j)),
                    pl.BlockSpec((None, tn, D_MODEL), lambda i, j: (layer, j, 0)),
                    row_spec(D_MODEL), vec_spec, vec_spec],
        out_specs=[row_spec(D_MODEL), row_spec(D_MODEL)],
        out_shape=[jax.ShapeDtypeStruct((t, D_MODEL), jnp.float32),
                   jax.ShapeDtypeStruct((t, D_MODEL), jnp.bfloat16)],
        scratch_shapes=[pltpu.VMEM((tm, D_MODEL), jnp.float32)],
        compiler_params=_cparams(("parallel", "arbitrary"), 56),
    )(xb, *ys, *([w_gate] * N_BRANCH), *([b_gate] * N_BRANCH), w_branch, w_out, xf, ln_g, ln_b)


def _router_kernel(x_ref, w_ref, b_ref, id_ref, wt_ref):
    x_hi, x_lo = _split_hi_lo(x_ref[...])
    w_hi, w_lo = w_ref[0], w_ref[1]
    logits = _bdot(x_hi, w_hi) + _bdot(x_lo, w_hi) + _bdot(x_hi, w_lo) + b_ref[...]
    lane = lax.broadcasted_iota(jnp.int32, logits.shape, 1)
    lane_f = lane.astype(jnp.float32)
    no_lane = float(LANES)

    def first_argmax(vals):
        mx = jnp.max(vals, axis=1, keepdims=True)
        idx = jnp.min(jnp.where(vals == mx, lane_f, no_lane), axis=1, keepdims=True)
        return mx, idx.astype(jnp.int32)

    is_group = lane < N_EGROUPS
    g_max, g_sel = first_argmax(jnp.where(is_group, logits, NEG))
    p_sel = 1.0 / jnp.sum(jnp.where(is_group, jnp.exp(logits - g_max), 0.0), axis=1, keepdims=True)

    lo = ROUTER_EOFF + g_sel * EXP_PER_GROUP
    cand = jnp.where((lane >= lo) & (lane < lo + EXP_PER_GROUP), logits, NEG)
    l1, i1 = first_argmax(cand)
    l2, i2 = first_argmax(jnp.where(lane == i1, NEG, cand))
    e = jnp.exp(l2 - l1)
    w1 = p_sel / (1.0 + e)
    w2 = p_sel * e / (1.0 + e)
    id_ref[...] = jnp.where(lane == 0, i1 - ROUTER_EOFF, jnp.where(lane == 1, i2 - ROUTER_EOFF, 0))
    wt_ref[...] = jnp.where(lane == 0, w1, jnp.where(lane == 1, w2, 0.0))


def _router(xf, w_r, b_r, layer, tm):
    t = xf.shape[0]
    out_spec = pl.BlockSpec((tm, LANES), lambda i: (i, 0))
    return pl.pallas_call(
        _router_kernel,
        grid=(t // tm,),
        in_specs=[pl.BlockSpec((tm, D_MODEL), lambda i: (i, 0)),
                  pl.BlockSpec((None, 2, D_MODEL, LANES), lambda i: (layer, 0, 0, 0)),
                  pl.BlockSpec((None, 1, LANES), lambda i: (layer, 0, 0))],
        out_specs=[out_spec, out_spec],
        out_shape=[jax.ShapeDtypeStruct((t, LANES), jnp.int32),
                   jax.ShapeDtypeStruct((t, LANES), jnp.float32)],
        compiler_params=_cparams(("parallel",), 32),
    )(xf, w_r, b_r)


def _dispatch(e_id, blk):
    t = e_id.shape[0]
    a = t * TOP_K
    e_flat = e_id.reshape(a)
    onehot = (e_flat[:, None] == jnp.arange(N_EXPERTS, dtype=jnp.int32)[None, :]).astype(jnp.int32)
    csum = jnp.cumsum(onehot, axis=0)
    counts = csum[-1]
    rank = jnp.sum(onehot * (csum - 1), axis=1)
    padded = (counts + blk - 1) // blk * blk
    pad_end = jnp.cumsum(padded)
    pad_start = pad_end - padded
    dest = jnp.sum(onehot * pad_start[None, :], axis=1) + rank
    nb = a // blk + N_EXPERTS
    slot_tok = jnp.zeros(nb * blk, jnp.int32).at[dest].set(jnp.arange(a, dtype=jnp.int32) // TOP_K)
    blk_e = jnp.minimum(jnp.searchsorted(pad_end, jnp.arange(nb, dtype=jnp.int32) * blk, side="right"),
                        N_EXPERTS - 1).astype(jnp.int32)
    n_used = (pad_end[-1] // blk).astype(jnp.int32).reshape(1)
    return slot_tok.reshape(nb, 1, blk), blk_e, n_used, dest.reshape(t, TOP_K).astype(jnp.int32)


def _row_copy(src_hbm, idx, buf, slot, r, sem):
    return pltpu.make_async_copy(src_hbm.at[pl.ds(idx, 1), :], buf.at[slot, pl.ds(r, 1), :], sem.at[slot])


def _start_row_gather(src_hbm, idx_ref, buf, slot, sem, n_rows):
    def body(r, carry):
        _row_copy(src_hbm, idx_ref[0, r], buf, slot, r, sem).start()
        return carry
    lax.fori_loop(0, n_rows, body, 0, unroll=8)


def _wait_row_gather(src_hbm, idx_ref, buf, slot, sem, n_rows):
    def body(r, carry):
        _row_copy(src_hbm, idx_ref[0, r], buf, slot, r, sem).wait()
        return carry
    lax.fori_loop(0, n_rows, body, 0, unroll=8)


def _expert_kernel(blk_e_ref, n_used_ref, tok_ref, tok_next_ref, x_hbm, w1_ref, w3_ref, w2_ref, y_ref,
                   xbuf, sem, w1_s, w3_s, w2_s):
    i = pl.program_id(0)
    n_used = n_used_ref[0]
    blk = xbuf.shape[1]
    slot = lax.rem(i, 2)

    @pl.when(i == 0)
    def _():
        _start_row_gather(x_hbm, tok_ref, xbuf, 0, sem, blk)

    @pl.when(i + 1 < n_used)
    def _():
        _start_row_gather(x_hbm, tok_next_ref, xbuf, 1 - slot, sem, blk)

    last = jnp.maximum(i - 1, 0)
    new_expert = jnp.logical_or(i == 0, blk_e_ref[i] != blk_e_ref[last])

    @pl.when(jnp.logical_and(i < n_used, new_expert))
    def _():
        w1_s[...] = w1_ref[...].astype(jnp.bfloat16)
        w3_s[...] = w3_ref[...].astype(jnp.bfloat16)
        w2_s[...] = w2_ref[...].astype(jnp.bfloat16)

    @pl.when(i < n_used)
    def _():
        _wait_row_gather(x_hbm, tok_ref, xbuf, slot, sem, blk)
        xb = xbuf[slot].astype(jnp.bfloat16)
        a = _bdot(xb, w1_s[...])
        h = (a * _sigmoid(a)) * _bdot(xb, w3_s[...])
        y_ref[...] = _bdot(h.astype(jnp.bfloat16), w2_s[...])

    @pl.when(i >= n_used)
    def _():
        y_ref[...] = jnp.zeros(y_ref.shape, y_ref.dtype)


def _experts(xf, slot_tok, blk_e, n_used, w1, w3, w2, layer):
    nb, _, blk = slot_tok.shape

    def w_index(i, be, nu):
        return (layer, be[jnp.minimum(i, nu[0] - 1)], 0, 0)

    grid_spec = pltpu.PrefetchScalarGridSpec(
        num_scalar_prefetch=2,
        grid=(nb,),
        in_specs=[pl.BlockSpec((None, 1, blk), lambda i, be, nu: (i, 0, 0), memory_space=pltpu.SMEM),
                  pl.BlockSpec((None, 1, blk), lambda i, be, nu: (jnp.minimum(i + 1, nb - 1), 0, 0),
                               memory_space=pltpu.SMEM),
                  pl.BlockSpec(memory_space=pl.ANY),
                  pl.BlockSpec((None, None, D_MODEL, D_EXPERT), w_index),
                  pl.BlockSpec((None, None, D_MODEL, D_EXPERT), w_index),
                  pl.BlockSpec((None, None, D_EXPERT, D_MODEL), w_index)],
        out_specs=pl.BlockSpec((blk, D_MODEL), lambda i, be, nu: (i, 0)),
        scratch_shapes=[pltpu.VMEM((2, blk, D_MODEL), jnp.float32),
                        pltpu.SemaphoreType.DMA((2,)),
                        pltpu.VMEM((D_MODEL, D_EXPERT), jnp.bfloat16),
                        pltpu.VMEM((D_MODEL, D_EXPERT), jnp.bfloat16),
                        pltpu.VMEM((D_EXPERT, D_MODEL), jnp.bfloat16)])
    return pl.pallas_call(
        _expert_kernel,
        grid_spec=grid_spec,
        out_shape=jax.ShapeDtypeStruct((nb * blk, D_MODEL), jnp.float32),
        compiler_params=_cparams(("arbitrary",), 56),
    )(blk_e, n_used, slot_tok, slot_tok, xf, w1, w3, w2)


def _combine_kernel(pos_ref, pos_next_ref, y_hbm, xf_ref, wt_ref, g_ref, b_ref, of_ref, ob_ref, ybuf, sem):
    i = pl.program_id(0)
    tm = xf_ref.shape[0]
    slot = lax.rem(i, 2)

    @pl.when(i == 0)
    def _():
        _start_row_gather(y_hbm, pos_ref, ybuf, 0, sem, TOP_K * tm)

    @pl.when(i + 1 < pl.num_programs(0))
    def _():
        _start_row_gather(y_hbm, pos_next_ref, ybuf, 1 - slot, sem, TOP_K * tm)

    _wait_row_gather(y_hbm, pos_ref, ybuf, slot, sem, TOP_K * tm)
    wt = wt_ref[...]
    moe = wt[:, 0:1] * ybuf[slot, pl.ds(0, tm), :] + wt[:, 1:2] * ybuf[slot, pl.ds(tm, tm), :]
    out = _layer_norm(ALPHA * xf_ref[...] + moe, g_ref[...], b_ref[...])
    of_ref[...] = out
    ob_ref[...] = out.astype(ob_ref.dtype)


def _combine(y_sorted, pos, xf, wts, ln_g, ln_b, layer, tm):
    t = xf.shape[0]
    nt = t // tm
    pos_tiles = pos.reshape(nt, tm, TOP_K).transpose(0, 2, 1).reshape(nt, 1, TOP_K * tm)
    row_spec = pl.BlockSpec((tm, D_MODEL), lambda i: (i, 0))
    vec_spec = pl.BlockSpec((None, 1, D_MODEL), lambda i: (layer, 0, 0))
    return pl.pallas_call(
        _combine_kernel,
        grid=(nt,),
        in_specs=[pl.BlockSpec((None, 1, TOP_K * tm), lambda i: (i, 0, 0), memory_space=pltpu.SMEM),
                  pl.BlockSpec((None, 1, TOP_K * tm), lambda i: (jnp.minimum(i + 1, nt - 1), 0, 0),
                               memory_space=pltpu.SMEM),
                  pl.BlockSpec(memory_space=pl.ANY),
                  row_spec,
                  pl.BlockSpec((tm, LANES), lambda i: (i, 0)),
                  vec_spec, vec_spec],
        out_specs=[row_spec, row_spec],
        out_shape=[jax.ShapeDtypeStruct((t, D_MODEL), jnp.float32),
                   jax.ShapeDtypeStruct((t, D_MODEL), jnp.bfloat16)],
        scratch_shapes=[pltpu.VMEM((2, TOP_K * tm, D_MODEL), jnp.float32),
                        pltpu.SemaphoreType.DMA((2,))],
        compiler_params=_cparams(("arbitrary",), 48),
    )(pos_tiles, pos_tiles, y_sorted, xf, wts, ln_g, ln_b)


MOE_BLOCK_ROWS = 256


def _mixer_layer(xf, xb, p, layer, batch, seq):
    t = batch * seq
    z_ml = _matmul(xb, p["w_ml"], layer, jnp.float32, 1024, 512)
    z_g = _matmul(xb, p["w_mg"], layer, jnp.float32, 1024, LANES)
    z_at = _matmul(xb, p["w_at"], layer, jnp.bfloat16, 1024, 512)
    z_sp = _matmul(xb, p["w_sp"], layer, jnp.float32, 1024, 512)
    y_ml = _mlstm(z_ml.reshape(batch, seq, ML_W), z_g.reshape(batch, seq, LANES),
                  p["ml_conv_w"], p["ml_gate_b"], p["ml_norm_w"], layer, batch, seq).reshape(t, BRANCH_W)
    z_at3 = z_at.reshape(batch, seq, 3 * ATT_W)
    groups = [_attention_group(z_at3, g, batch, seq) for g in range(ATT_NGROUP)]
    y_at = _attn_combine([o for o, _ in groups], [l for _, l in groups], min(1024, t))
    y_sg = _spatial_gating(z_sp, p["sg_ln_g"], p["sg_ln_b"], p["sg_w"], p["sg_b"], layer, 512)
    y_pl = _multiscale_pool(z_sp.reshape(batch, seq, 3 * BRANCH_W), p["pool_w"], p["pool_scale"],
                            layer, batch, seq).reshape(t, BRANCH_W)
    return _merge(xb, xf, (y_ml, y_at, y_sg, y_pl), p["w_gate"], p["b_gate"], p["w_branch"], p["w_out"],
                  p["ln1_g"], p["ln1_b"], layer, 512, 256)


def _moe_layer(xf, p, layer):
    ids, wts = _router(xf, p["w_router"], p["b_router"], layer, 512)
    slot_tok, blk_e, n_used, pos = _dispatch(ids[:, :TOP_K], MOE_BLOCK_ROWS)
    y_sorted = _experts(xf, slot_tok, blk_e, n_used, p["w_exp_gate"], p["w_exp_up"], p["w_exp_down"], layer)
    return _combine(y_sorted, pos, xf, wts, p["ln2_g"], p["ln2_b"], layer, 256)


def _prepare_params(w_in, ml_conv_w, ml_gate_b, ml_norm_w, sg_ln_g, sg_ln_b, sg_w, sg_b, pool_w, pool_scale,
                    w_gate, b_gate, w_branch, w_out, ln1_g, ln1_b, w_router_group, b_router_group,
                    w_router_expert, b_router_expert, w_exp_gate, w_exp_up, w_exp_down, ln2_g, ln2_b):
    bf = jnp.bfloat16
    depth = w_in.shape[0]
    c_ml, c_g = ML_W, ML_W + ML_GATES
    c_at = c_g + 3 * ATT_W
    row = lambda a: a.reshape(depth, 1, -1)
    lane_pad = lambda a: jnp.pad(a, [(0, 0)] * (a.ndim - 1) + [(0, LANES - a.shape[-1])])
    w_r = jnp.zeros((depth, D_MODEL, LANES), jnp.float32)
    w_r = w_r.at[:, :, :N_EGROUPS].set(w_router_group).at[:, :, ROUTER_EOFF:ROUTER_EOFF + N_EXPERTS].set(w_router_expert)
    b_r = jnp.zeros((depth, LANES), jnp.float32)
    b_r = b_r.at[:, :N_EGROUPS].set(b_router_group).at[:, ROUTER_EOFF:ROUTER_EOFF + N_EXPERTS].set(b_router_expert)
    w_r_hi = w_r.astype(bf)
    w_r_lo = (w_r - w_r_hi.astype(jnp.float32)).astype(bf)
    sg_b_exp = jnp.repeat(jnp.swapaxes(sg_b, 1, 2), BRANCH_W // SG_GROUPS, axis=2)
    return dict(
        w_ml=w_in[:, :, :c_ml].astype(bf),
        w_mg=lane_pad(w_in[:, :, c_ml:c_g]).astype(bf),
        w_at=w_in[:, :, c_g:c_at].astype(bf),
        w_sp=w_in[:, :, c_at:].astype(bf),
        ml_conv_w=ml_conv_w,
        ml_gate_b=lane_pad(ml_gate_b.reshape(depth, 1, ML_GATES)),
        ml_norm_w=row(ml_norm_w),
        sg_ln_g=row(sg_ln_g), sg_ln_b=row(sg_ln_b), sg_w=sg_w.astype(bf), sg_b=sg_b_exp,
        pool_w=pool_w.astype(bf), pool_scale=row(pool_scale),
        w_gate=w_gate.astype(bf), b_gate=row(b_gate), w_branch=w_branch.astype(bf), w_out=w_out.astype(bf),
        ln1_g=row(ln1_g), ln1_b=row(ln1_b),
        w_router=jnp.stack([w_r_hi, w_r_lo], axis=1), b_router=row(b_r),
        w_exp_gate=w_exp_gate, w_exp_up=w_exp_up, w_exp_down=w_exp_down,
        ln2_g=row(ln2_g), ln2_b=row(ln2_b))


def kernel(x, w_in, ml_conv_w, ml_gate_b, ml_norm_w, sg_ln_g, sg_ln_b, sg_w, sg_b, pool_w, pool_scale, w_gate,
           b_gate, w_branch, w_out, ln1_g, ln1_b, w_router_group, b_router_group, w_router_expert,
           b_router_expert, w_exp_gate, w_exp_up, w_exp_down, ln2_g, ln2_b):
    batch, seq, d = x.shape
    assert d == D_MODEL and seq % (ATT_DIL[-1] * ATT_QT) == 0
    p = _prepare_params(w_in, ml_conv_w, ml_gate_b, ml_norm_w, sg_ln_g, sg_ln_b, sg_w, sg_b, pool_w, pool_scale,
                        w_gate, b_gate, w_branch, w_out, ln1_g, ln1_b, w_router_group, b_router_group,
                        w_router_expert, b_router_expert, w_exp_gate, w_exp_up, w_exp_down, ln2_g, ln2_b)
    xf = x.reshape(batch * seq, d)
    xb = xf.astype(jnp.bfloat16)
    for layer in range(w_in.shape[0]):
        xf, xb = _mixer_layer(xf, xb, p, layer, batch, seq)
        xf, xb = _moe_layer(xf, p, layer)
    return xf.reshape(batch, seq, d)
```

```python
import functools

import numpy as np
import jax
import jax.numpy as jnp
from jax import lax
from jax.experimental import pallas as pl
from jax.experimental.pallas import tpu as pltpu

D_MODEL = 2048
DEPTH = 4
BRANCH_W = 512
N_BRANCH = 4

ML_HEADS = 4
ML_DQK = 64
ML_DV = 128
ML_CHUNK = 128
ML_QK_W = ML_HEADS * ML_DQK
ML_V_W = ML_HEADS * ML_DV
ML_GATES = 2 * 2 * ML_HEADS
ML_W = 2 * ML_QK_W + 2 * ML_V_W

ATT_WINDOW = (128, 512, 2048)
ATT_DIL = (1, 4, 16)
ATT_NGROUP = 3
ATT_HEADS = 8
ATT_DH = 64
ATT_GW = ATT_HEADS * ATT_DH
ATT_W = ATT_NGROUP * ATT_GW
ATT_NEIGH = 64
ATT_QT = 128

SG_CHUNK = 128
SG_GROUPS = 4
POOL_WINDOWS = (2, 4, 8, 16)
POOL_DG = BRANCH_W // len(POOL_WINDOWS)

N_EGROUPS = 4
EXP_PER_GROUP = 8
N_EXPERTS = N_EGROUPS * EXP_PER_GROUP
TOP_K = 2
D_EXPERT = 512
ROUTER_EOFF = 32

ALPHA = (2.0 * DEPTH) ** 0.25
LN_EPS = 1e-5
NEG = -1e30

LANES = 128
VMEM_BYTES_V7X = 64 * 1024 * 1024


def _cparams(semantics, vmem_mb):
    assert vmem_mb * 1024 * 1024 < VMEM_BYTES_V7X
    return pltpu.CompilerParams(dimension_semantics=semantics, vmem_limit_bytes=vmem_mb * 1024 * 1024)


def _sigmoid(x):
    return 1.0 / (1.0 + jnp.exp(-x))


def _bdot(a, b):
    return jnp.dot(a, b, preferred_element_type=jnp.float32)


def _layer_norm(x, g, b):
    mu = jnp.mean(x, axis=-1, keepdims=True)
    xc = x - mu
    var = jnp.mean(xc * xc, axis=-1, keepdims=True)
    return xc * lax.rsqrt(var + LN_EPS) * g + b


def _mm_kernel(x_ref, w_ref, o_ref):
    o_ref[...] = _bdot(x_ref[...], w_ref[...]).astype(o_ref.dtype)


def _matmul(x, w, layer, out_dtype, tm, tn):
    t, k = x.shape
    n = w.shape[2]
    return pl.pallas_call(
        _mm_kernel,
        grid=(t // tm, n // tn),
        in_specs=[pl.BlockSpec((tm, k), lambda i, j: (i, 0)),
                  pl.BlockSpec((None, k, tn), lambda i, j: (layer, 0, j))],
        out_specs=pl.BlockSpec((tm, tn), lambda i, j: (i, j)),
        out_shape=jax.ShapeDtypeStruct((t, n), out_dtype),
        compiler_params=_cparams(("parallel", "arbitrary"), 40),
    )(x, w)


def _split_hi_lo(x):
    hi = x.astype(jnp.bfloat16)
    lo = (x - hi.astype(jnp.float32)).astype(jnp.bfloat16)
    return hi, lo


def _mlstm_kernel(q_ref, k_ref, v_ref, o_ref, g_ref, cw_ref, gb_ref, nw_ref, y_ref,
                  qc_s, kc_s, gp_s, hf_s, hb_s, ct_s):
    s_len = q_ref.shape[0]
    n_chunks = s_len // ML_CHUNK
    lc = ML_CHUNK

    row_c = lax.broadcasted_iota(jnp.int32, (lc, 1), 0)
    cw = cw_ref[...]
    gate_lane = lax.broadcasted_iota(jnp.int32, (lc, LANES), 1)

    def prep(c, carry):
        r0 = pl.multiple_of(c * lc, lc)
        rows = pl.ds(r0, lc)

        def conv_silu(ref, w):
            x = ref[rows, :]
            prev_row = jnp.where(c > 0, ref[pl.ds(jnp.maximum(r0 - 1, 0), 1), :], 0.0)
            next_row = jnp.where(c < n_chunks - 1, ref[pl.ds(jnp.minimum(r0 + lc, s_len - 1), 1), :], 0.0)
            x_prev = jnp.where(row_c == 0, prev_row, pltpu.roll(x, 1, 0))
            x_next = jnp.where(row_c == lc - 1, next_row, pltpu.roll(x, lc - 1, 0))
            y = x_prev * w[0:1] + x * w[1:2] + x_next * w[2:3]
            return y * _sigmoid(y)

        qc_s[rows, :] = (conv_silu(q_ref, cw[:, :ML_QK_W]) * (ML_DQK ** -0.5)).astype(jnp.bfloat16)
        kc_s[rows, :] = conv_silu(k_ref, cw[:, ML_QK_W:]).astype(jnp.bfloat16)

        gp = g_ref[rows, :] + gb_ref[...]
        log_sig = jnp.minimum(gp, 0.0) - jnp.log(1.0 + jnp.exp(-jnp.abs(gp)))
        gp_s[rows, :] = jnp.where((gate_lane % 8) >= ML_HEADS, log_sig, gp)
        return carry

    lax.fori_loop(0, n_chunks, prep, 0)

    r_i = lax.broadcasted_iota(jnp.int32, (lc, lc), 0)
    c_i = lax.broadcasted_iota(jnp.int32, (lc, lc), 1)
    lower = c_i <= r_i
    upper = c_i >= r_i
    tril = jnp.where(lower, 1.0, 0.0).astype(jnp.bfloat16)
    triu = jnp.where(upper, 1.0, 0.0).astype(jnp.bfloat16)
    ones_ext = jnp.ones((lc, ML_DV), jnp.bfloat16)
    head_of_lane = lax.broadcasted_iota(jnp.int32, (lc, ML_QK_W), 1) // ML_DQK

    ct_s[...] = jnp.zeros(ct_s.shape, ct_s.dtype)

    def chunk_dir(c, direction, ms, h_out):
        r0 = pl.multiple_of(c * lc, lc)
        rows = pl.ds(r0, lc)
        gc = gp_s[rows, :]
        gct = gc.T
        gc_hi, gc_lo = _split_hi_lo(gc)
        gct_hi, gct_lo = _split_hi_lo(gct)
        if direction == 0:
            cum_col = _bdot(tril, gc_hi) + _bdot(tril, gc_lo)
            cum_row = _bdot(gct_hi, triu) + _bdot(gct_lo, triu)
            mask = lower
        else:
            cum_col = _bdot(triu, gc_hi) + _bdot(triu, gc_lo)
            cum_row = _bdot(gct_hi, tril) + _bdot(gct_lo, tril)
            mask = upper
        qch = qc_s[rows, :]
        kch = kc_s[rows, :]
        vch = v_ref[rows, :].astype(jnp.bfloat16)
        kf = kch.astype(jnp.float32)

        new_ms = []
        wk_cols = []
        decays = []
        for h in range(ML_HEADS):
            ci = direction * 8 + h
            cf = ci + ML_HEADS
            m = ms[h]
            b_col = cum_col[:, cf:cf + 1]
            b_row = cum_row[cf:cf + 1, :]
            i_row = gct[ci:ci + 1, :]
            i_col = gc[:, ci:ci + 1]
            total = b_row[:, lc - 1:lc] if direction == 0 else b_row[:, 0:1]
            dmat = jnp.where(mask, b_col - b_row + i_row, NEG)
            m_inter = m + b_col
            m_t = jnp.maximum(m_inter, jnp.max(dmat, axis=1, keepdims=True))
            w_inter = jnp.exp(m_inter - m_t)
            qh = qch[:, h * ML_DQK:(h + 1) * ML_DQK]
            kh = kch[:, h * ML_DQK:(h + 1) * ML_DQK]
            v_ext = jnp.concatenate([vch[:, h * ML_DV:(h + 1) * ML_DV], ones_ext], axis=1)
            s = lax.dot_general(qh, kh, (((1,), (1,)), ((), ())),
                                preferred_element_type=jnp.float32) * jnp.exp(dmat - m_t)
            ct = ct_s[direction * ML_HEADS + h]
            both = w_inter * _bdot(qh, ct.astype(jnp.bfloat16)) + _bdot(s.astype(jnp.bfloat16), v_ext)
            num = both[:, :ML_DV]
            den = both[:, ML_DV:]
            h_out[rows, h * ML_DV:(h + 1) * ML_DV] = num / jnp.maximum(jnp.abs(den), jnp.exp(-m_t))

            g_row = total - b_row + i_row
            m_new = jnp.maximum(m + total, jnp.max(g_row, axis=1, keepdims=True))
            decays.append(jnp.exp(m + total - m_new))
            wk_cols.append(jnp.exp(total - b_col + i_col - m_new))
            new_ms.append(m_new)

        wk_all = wk_cols[0]
        for h in range(1, ML_HEADS):
            wk_all = jnp.where(head_of_lane >= h, wk_cols[h], wk_all)
        kwt = (kf * wk_all).T.astype(jnp.bfloat16)
        for h in range(ML_HEADS):
            v_ext = jnp.concatenate([vch[:, h * ML_DV:(h + 1) * ML_DV], ones_ext], axis=1)
            idx = direction * ML_HEADS + h
            ct_s[idx] = decays[h] * ct_s[idx] + _bdot(kwt[h * ML_DQK:(h + 1) * ML_DQK, :], v_ext)
        return new_ms

    def body(c, carry):
        ms_f = chunk_dir(c, 0, list(carry[:ML_HEADS]), hf_s)
        ms_b = chunk_dir(n_chunks - 1 - c, 1, list(carry[ML_HEADS:]), hb_s)
        return tuple(ms_f) + tuple(ms_b)

    zero = jnp.zeros((1, 1), jnp.float32)
    lax.fori_loop(0, n_chunks, body, (zero,) * (2 * ML_HEADS))

    nw = nw_ref[...]

    def finish(c, carry):
        rows = pl.ds(pl.multiple_of(c * lc, lc), lc)
        hsum = hf_s[rows, :] + hb_s[rows, :]
        og = _sigmoid(o_ref[rows, :])
        for h in range(ML_HEADS):
            sl = slice(h * ML_DV, (h + 1) * ML_DV)
            hh = hsum[:, sl]
            mu = jnp.mean(hh, axis=-1, keepdims=True)
            hc = hh - mu
            var = jnp.mean(hc * hc, axis=-1, keepdims=True)
            y_ref[rows, sl] = (hc * lax.rsqrt(var + LN_EPS) * nw[:, sl] * og[:, sl]).astype(y_ref.dtype)
        return carry

    lax.fori_loop(0, n_chunks, finish, 0)


def _mlstm(z_ml, z_g, conv_w, gate_b, norm_w, layer, batch, seq):
    return pl.pallas_call(
        _mlstm_kernel,
        grid=(batch,),
        in_specs=[pl.BlockSpec((None, seq, ML_QK_W), lambda b: (b, 0, 0)),
                  pl.BlockSpec((None, seq, ML_QK_W), lambda b: (b, 0, 1)),
                  pl.BlockSpec((None, seq, ML_V_W), lambda b: (b, 0, 1)),
                  pl.BlockSpec((None, seq, ML_V_W), lambda b: (b, 0, 2)),
                  pl.BlockSpec((None, seq, LANES), lambda b: (b, 0, 0)),
                  pl.BlockSpec((None, 3, 2 * ML_QK_W), lambda b: (layer, 0, 0)),
                  pl.BlockSpec((None, 1, LANES), lambda b: (layer, 0, 0)),
                  pl.BlockSpec((None, 1, ML_V_W), lambda b: (layer, 0, 0))],
        out_specs=pl.BlockSpec((None, seq, ML_V_W), lambda b: (b, 0, 0)),
        out_shape=jax.ShapeDtypeStruct((batch, seq, ML_V_W), jnp.bfloat16),
        scratch_shapes=[pltpu.VMEM((seq, ML_QK_W), jnp.bfloat16),
                        pltpu.VMEM((seq, ML_QK_W), jnp.bfloat16),
                        pltpu.VMEM((seq, LANES), jnp.float32),
                        pltpu.VMEM((seq, ML_V_W), jnp.float32),
                        pltpu.VMEM((seq, ML_V_W), jnp.float32),
                        pltpu.VMEM((2 * ML_HEADS, ML_DQK, 2 * ML_DV), jnp.float32)],
        compiler_params=_cparams(("parallel",), 56),
    )(z_ml, z_ml, z_ml, z_ml, z_g, conv_w, gate_b, norm_w)


def _alibi_slopes():
    n = ATT_NGROUP * ATT_HEADS
    s = 2.0 ** (-8.0 * np.arange(1, n + 1) / n)
    return s.reshape(ATT_NGROUP, ATT_HEADS).astype(np.float32)


ATT_SLABS = ATT_GW // LANES
ATT_STAGE_ROWS = 256


def _attn_group(q_ref, k_ref, v_ref, y_ref, stage_s, qs, ks, vs, oacc, lacc, *, group, seq, first, last):
    dil = ATT_DIL[group]
    assert ATT_WINDOW[group] // (2 * dil) == ATT_NEIGH and (not last or dil == 1)
    sub_len = seq // dil
    slopes = [float(s) for s in _alibi_slopes()[group] * np.float32(dil)]
    kw = min(2 * ATT_QT, sub_len)
    tiles_per_sub = sub_len // ATT_QT

    if dil == 1:
        q_src, k_src, v_src = q_ref, k_ref, v_ref
    else:
        for src, dst in ((q_ref, qs), (k_ref, ks), (v_ref, vs)):
            def stage(c, carry, src=src):
                rows = pl.ds(pl.multiple_of(c * ATT_STAGE_ROWS, ATT_STAGE_ROWS), ATT_STAGE_ROWS)
                x = src[rows, :].astype(jnp.float32)
                for sl in range(ATT_SLABS):
                    stage_s[sl, rows, :] = x[:, sl * LANES:(sl + 1) * LANES]
                return carry

            lax.fori_loop(0, seq // ATT_STAGE_ROWS, stage, 0)

            def gather_residue(r, carry, dst=dst):
                out_rows = pl.ds(pl.multiple_of(r * sub_len, sub_len), sub_len)
                for sl in range(ATT_SLABS):
                    x = stage_s[sl, pl.ds(r, sub_len, stride=dil), :]
                    dst[out_rows, sl * LANES:(sl + 1) * LANES] = x.astype(jnp.bfloat16)
                return carry

            lax.fori_loop(0, dil, gather_residue, 0)
        q_src, k_src, v_src = qs, ks, vs

    def tile(idx, carry):
        r = idx // tiles_per_sub
        q0 = (idx % tiles_per_sub) * ATT_QT
        k0 = jnp.clip(q0 - ATT_NEIGH, 0, sub_len - kw)
        base = r * sub_len
        qpos = q0 + lax.broadcasted_iota(jnp.int32, (ATT_QT, 1), 0)
        kpos = k0 + lax.broadcasted_iota(jnp.int32, (1, kw), 1)
        arel = jnp.abs(kpos - qpos).astype(jnp.float32)
        valid = arel <= float(ATT_NEIGH)
        q = q_src[pl.ds(pl.multiple_of(base + q0, ATT_QT), ATT_QT), :]
        k = k_src[pl.ds(pl.multiple_of(base + k0, ATT_NEIGH), kw), :]
        v = v_src[pl.ds(pl.multiple_of(base + k0, ATT_NEIGH), kw), :]
        o_parts, l_parts = [], []
        for h in range(ATT_HEADS):
            sl = slice(h * ATT_DH, (h + 1) * ATT_DH)
            s = lax.dot_general(q[:, sl], k[:, sl], (((1,), (1,)), ((), ())),
                                preferred_element_type=jnp.float32) * (ATT_DH ** -0.5)
            s = jnp.where(valid, s - slopes[h] * arel, NEG)
            m = jnp.max(s, axis=1, keepdims=True)
            p = jnp.exp(s - m)
            den = jnp.sum(p, axis=1, keepdims=True)
            o_parts.append(_bdot(p.astype(jnp.bfloat16), v[:, sl]) / den)
            l_parts.append(jnp.broadcast_to(m + jnp.log(den), (ATT_QT, ATT_DH)))
        o_new = jnp.concatenate(o_parts, axis=1)
        l_new = jnp.concatenate(l_parts, axis=1)

        start = q0 * dil + r
        nat_rows = pl.ds(start, ATT_QT) if dil == 1 else pl.ds(start, ATT_QT, stride=dil)
        for sl in range(ATT_SLABS):
            cols = slice(sl * LANES, (sl + 1) * LANES)
            o_sl, l_sl = o_new[:, cols], l_new[:, cols]
            if not first:
                o_old, l_old = oacc[sl, nat_rows, :], lacc[sl, nat_rows, :]
                mx = jnp.maximum(l_old, l_sl)
                w_old, w_new = jnp.exp(l_old - mx), jnp.exp(l_sl - mx)
                tot = w_old + w_new
                o_sl = (w_old * o_old + w_new * o_sl) / tot
                l_sl = mx + jnp.log(tot)
            if last:
                y_ref[pl.ds(pl.multiple_of(start, ATT_QT), ATT_QT), cols] = o_sl.astype(y_ref.dtype)
            else:
                oacc[sl, nat_rows, :] = o_sl
                lacc[sl, nat_rows, :] = l_sl
        return carry

    lax.fori_loop(0, seq // ATT_QT, tile, 0)


def _attn_kernel(q_ref, k_ref, v_ref, y_ref, stage_s, qs, ks, vs, oacc, lacc, *, seq):
    gi = pl.program_id(1)
    scratch = (stage_s, qs, ks, vs, oacc, lacc)
    for step in range(ATT_NGROUP):
        @pl.when(gi == step)
        def _(step=step):
            _attn_group(q_ref, k_ref, v_ref, y_ref, *scratch, group=ATT_NGROUP - 1 - step, seq=seq,
                        first=step == 0, last=step == ATT_NGROUP - 1)


def _dilated_attention(z_at, batch, seq):
    def in_spec(which):
        return pl.BlockSpec((None, seq, ATT_GW),
                            lambda b, gi: (b, 0, which * ATT_NGROUP + ATT_NGROUP - 1 - gi))

    slab = pltpu.VMEM((ATT_SLABS, seq, LANES), jnp.float32)
    sub = pltpu.VMEM((seq, ATT_GW), jnp.bfloat16)
    return pl.pallas_call(
        functools.partial(_attn_kernel, seq=seq),
        grid=(batch, ATT_NGROUP),
        in_specs=[in_spec(0), in_spec(1), in_spec(2)],
        out_specs=pl.BlockSpec((None, seq, ATT_GW), lambda b, gi: (b, 0, 0)),
        out_shape=jax.ShapeDtypeStruct((batch, seq, ATT_GW), jnp.bfloat16),
        scratch_shapes=[slab, sub, sub, sub, slab, slab],
        compiler_params=_cparams(("parallel", "arbitrary"), 48),
    )(z_at, z_at, z_at)


def _gelu_tanh(x):
    return x * (0.5 * (1.0 + jnp.tanh(np.sqrt(2.0 / np.pi).astype(np.float32) * (x + 0.044715 * (x * x * x)))))


def _sg_kernel(u_ref, v_ref, g_ref, b_ref, ws_ref, bs_ref, y_ref):
    tm = u_ref.shape[0]
    dg = BRANCH_W // SG_GROUPS
    u = _gelu_tanh(u_ref[...])
    vn = _layer_norm(_gelu_tanh(v_ref[...]), g_ref[...], b_ref[...]).astype(jnp.bfloat16)
    bs = bs_ref[...]
    for c in range(tm // SG_CHUNK):
        rows = slice(c * SG_CHUNK, (c + 1) * SG_CHUNK)
        for g in range(SG_GROUPS):
            cols = slice(g * dg, (g + 1) * dg)
            mixed = _bdot(ws_ref[g], vn[rows, cols]) + bs[:, cols]
            y_ref[rows, cols] = (u[rows, cols] * mixed).astype(y_ref.dtype)


def _spatial_gating(z_sp, ln_g, ln_b, w_s, b_s_exp, layer, tm):
    t = z_sp.shape[0]
    return pl.pallas_call(
        _sg_kernel,
        grid=(t // tm,),
        in_specs=[pl.BlockSpec((tm, BRANCH_W), lambda i: (i, 0)),
                  pl.BlockSpec((tm, BRANCH_W), lambda i: (i, 1)),
                  pl.BlockSpec((None, 1, BRANCH_W), lambda i: (layer, 0, 0)),
                  pl.BlockSpec((None, 1, BRANCH_W), lambda i: (layer, 0, 0)),
                  pl.BlockSpec((None, SG_GROUPS, SG_CHUNK, SG_CHUNK), lambda i: (layer, 0, 0, 0)),
                  pl.BlockSpec((None, SG_CHUNK, BRANCH_W), lambda i: (layer, 0, 0))],
        out_specs=pl.BlockSpec((tm, BRANCH_W), lambda i: (i, 0)),
        out_shape=jax.ShapeDtypeStruct((t, BRANCH_W), jnp.bfloat16),
        compiler_params=_cparams(("parallel",), 32),
    )(z_sp, z_sp, ln_g, ln_b, w_s, b_s_exp)


def _pool_kernel(p_ref, w_ref, sc_ref, y_ref):
    s_len = p_ref.shape[0]
    row = lax.broadcasted_iota(jnp.int32, (s_len, 1), 0)
    sc = sc_ref[...]
    for g, win in enumerate(POOL_WINDOWS):
        cols = slice(g * POOL_DG, (g + 1) * POOL_DG)
        half = win // 2
        p = p_ref[:, cols]
        acc = jnp.zeros_like(p)
        for j in range(-half, half):
            if j == 0:
                acc = acc + p
            else:
                shifted = pltpu.roll(p, (-j) % s_len, 0)
                ok = (row + j >= 0) & (row + j < s_len)
                acc = acc + jnp.where(ok, shifted, 0.0)
        cnt = (jnp.minimum(row + half, s_len) - jnp.maximum(row - half, 0)).astype(jnp.float32)
        d = acc / cnt - p
        y = _bdot(d.astype(jnp.bfloat16), w_ref[g]) * sc[:, cols]
        y_ref[:, cols] = y.astype(y_ref.dtype)


def _multiscale_pool(z_sp, pool_w, pool_scale, layer, batch, seq):
    return pl.pallas_call(
        _pool_kernel,
        grid=(batch,),
        in_specs=[pl.BlockSpec((None, seq, BRANCH_W), lambda b: (b, 0, 2)),
                  pl.BlockSpec((None, len(POOL_WINDOWS), POOL_DG, POOL_DG), lambda b: (layer, 0, 0, 0)),
                  pl.BlockSpec((None, 1, BRANCH_W), lambda b: (layer, 0, 0))],
        out_specs=pl.BlockSpec((None, seq, BRANCH_W), lambda b: (b, 0, 0)),
        out_shape=jax.ShapeDtypeStruct((batch, seq, BRANCH_W), jnp.bfloat16),
        compiler_params=_cparams(("parallel",), 48),
    )(z_sp, pool_w, pool_scale)


def _merge_kernel(xb_ref, y0, y1, y2, y3, wg0, wg1, wg2, wg3, bg0, bg1, bg2, bg3, wb_ref, wo_ref,
                  xf_ref, g_ref, b_ref, of_ref, ob_ref, acc_ref):
    j = pl.program_id(1)
    xb = xb_ref[...]
    merged = None
    for n, (y, wg, bg) in enumerate(((y0, wg0, bg0), (y1, wg1, bg1), (y2, wg2, bg2), (y3, wg3, bg3))):
        gate = _sigmoid(_bdot(xb, wg[...]) + bg[...])
        term = gate * _bdot(y[...], wb_ref[n])
        merged = term if merged is None else merged + term
    part = _bdot(merged.astype(jnp.bfloat16), wo_ref[...])

    @pl.when(j == 0)
    def _():
        acc_ref[...] = part

    @pl.when(j > 0)
    def _():
        acc_ref[...] += part

    @pl.when(j == pl.num_programs(1) - 1)
    def _():
        out = _layer_norm(ALPHA * xf_ref[...] + acc_ref[...], g_ref[...], b_ref[...])
        of_ref[...] = out
        ob_ref[...] = out.astype(ob_ref.dtype)


def _merge(xb, xf, ys, w_gate, b_gate, w_branch, w_out, ln_g, ln_b, layer, tm, tn):
    t = xb.shape[0]
    nj = D_MODEL // tn
    row_spec = lambda w: pl.BlockSpec((tm, w), lambda i, j: (i, 0))
    wg_specs = [pl.BlockSpec((None, D_MODEL, tn), lambda i, j, n=n: (layer, 0, n * nj + j)) for n in range(N_BRANCH)]
    bg_specs = [pl.BlockSpec((None, 1, tn), lambda i, j, n=n: (layer, 0, n * nj + j)) for n in range(N_BRANCH)]
    vec_spec = pl.BlockSpec((None, 1, D_MODEL), lambda i, j: (layer, 0, 0))
    return pl.pallas_call(
        _merge_kernel,
        grid=(t // tm, nj),
        in_specs=[row_spec(D_MODEL)] + [row_spec(BRANCH_W)] * N_BRANCH + wg_specs + bg_specs
                 + [pl.BlockSpec((None, N_BRANCH, BRANCH_W, tn), lambda i, j: (layer, 0, 0, j)),
                    pl.BlockSpec((None, tn, D_MODEL), lambda i, j: (layer, j, 0)),
                    row_spec(D_MODEL), vec_spec, vec_spec],
        out_specs=[row_spec(D_MODEL), row_spec(D_MODEL)],
        out_shape=[jax.ShapeDtypeStruct((t, D_MODEL), jnp.float32),
                   jax.ShapeDtypeStruct((t, D_MODEL), jnp.bfloat16)],
        scratch_shapes=[pltpu.VMEM((tm, D_MODEL), jnp.float32)],
        compiler_params=_cparams(("parallel", "arbitrary"), 56),
    )(xb, *ys, *([w_gate] * N_BRANCH), *([b_gate] * N_BRANCH), w_branch, w_out, xf, ln_g, ln_b)


def _router_kernel(x_ref, w_ref, b_ref, id_ref, wt_ref, cnt_ref):
    x_hi, x_lo = _split_hi_lo(x_ref[...])
    w_hi, w_lo = w_ref[0], w_ref[1]
    logits = _bdot(x_hi, w_hi) + _bdot(x_lo, w_hi) + _bdot(x_hi, w_lo) + b_ref[...]
    lane = lax.broadcasted_iota(jnp.int32, logits.shape, 1)
    lane_f = lane.astype(jnp.float32)
    no_lane = float(LANES)

    def first_argmax(vals):
        mx = jnp.max(vals, axis=1, keepdims=True)
        idx = jnp.min(jnp.where(vals == mx, lane_f, no_lane), axis=1, keepdims=True)
        return mx, idx.astype(jnp.int32)

    is_group = lane < N_EGROUPS
    g_max, g_sel = first_argmax(jnp.where(is_group, logits, NEG))
    p_sel = 1.0 / jnp.sum(jnp.where(is_group, jnp.exp(logits - g_max), 0.0), axis=1, keepdims=True)

    lo = ROUTER_EOFF + g_sel * EXP_PER_GROUP
    cand = jnp.where((lane >= lo) & (lane < lo + EXP_PER_GROUP), logits, NEG)
    l1, i1 = first_argmax(cand)
    l2, i2 = first_argmax(jnp.where(lane == i1, NEG, cand))
    e = jnp.exp(l2 - l1)
    w1 = p_sel / (1.0 + e)
    w2 = p_sel * e / (1.0 + e)
    e1, e2 = i1 - ROUTER_EOFF, i2 - ROUTER_EOFF

    tm = logits.shape[0]
    hit1, hit2 = lane == e1, lane == e2
    hits = jnp.where(jnp.logical_or(hit1, hit2), 1.0, 0.0)
    earlier = (lax.broadcasted_iota(jnp.int32, (tm, tm), 1) < lax.broadcasted_iota(jnp.int32, (tm, tm), 0))
    before = _bdot(jnp.where(earlier, 1.0, 0.0).astype(jnp.bfloat16), hits.astype(jnp.bfloat16))
    rank1 = jnp.sum(jnp.where(hit1, before, 0.0), axis=1, keepdims=True).astype(jnp.int32)
    rank2 = jnp.sum(jnp.where(hit2, before, 0.0), axis=1, keepdims=True).astype(jnp.int32)
    counts = jnp.sum(hits, axis=0, keepdims=True).astype(jnp.int32)

    id_ref[...] = jnp.where(lane == 0, e1, jnp.where(lane == 1, e2,
                            jnp.where(lane == 2, rank1, jnp.where(lane == 3, rank2, 0))))
    wt_ref[...] = jnp.where(lane == 0, w1, jnp.where(lane == 1, w2, 0.0))
    cnt_ref[...] = jnp.broadcast_to(counts, cnt_ref.shape)


def _router(xf, w_r, b_r, layer, tm):
    t = xf.shape[0]
    out_spec = pl.BlockSpec((tm, LANES), lambda i: (i, 0))
    return pl.pallas_call(
        _router_kernel,
        grid=(t // tm,),
        in_specs=[pl.BlockSpec((tm, D_MODEL), lambda i: (i, 0)),
                  pl.BlockSpec((None, 2, D_MODEL, LANES), lambda i: (layer, 0, 0, 0)),
                  pl.BlockSpec((None, 1, LANES), lambda i: (layer, 0, 0))],
        out_specs=[out_spec, out_spec, pl.BlockSpec((None, 8, LANES), lambda i: (i, 0, 0))],
        out_shape=[jax.ShapeDtypeStruct((t, LANES), jnp.int32),
                   jax.ShapeDtypeStruct((t, LANES), jnp.float32),
                   jax.ShapeDtypeStruct((t // tm, 8, LANES), jnp.int32)],
        compiler_params=_cparams(("parallel",), 32),
    )(xf, w_r, b_r)


def _dispatch_plan(info, counts, blk, tm):
    t = info.shape[0]
    c = counts[:, 0, :N_EXPERTS]
    tile_off = jnp.cumsum(c, axis=0) - c
    total = jnp.sum(c, axis=0)
    padded = (total + blk - 1) // blk * blk
    pad_end = jnp.cumsum(padded)
    base = (pad_end - padded)[None, :] + tile_off
    base_tok = jnp.repeat(base, tm, axis=0)
    hit = info[:, :TOP_K, None] == jnp.arange(N_EXPERTS, dtype=jnp.int32)[None, None, :]
    dest = jnp.sum(jnp.where(hit, base_tok[:, None, :], 0), axis=2) + info[:, TOP_K:2 * TOP_K]
    nb = t * TOP_K // blk + N_EXPERTS
    blk_e = jnp.minimum(jnp.searchsorted(pad_end, jnp.arange(nb, dtype=jnp.int32) * blk, side="right"),
                        N_EXPERTS - 1).astype(jnp.int32)
    n_used = (pad_end[-1] // blk).astype(jnp.int32).reshape(1)
    return dest.astype(jnp.int32), blk_e, n_used


def _dispatch_kernel(dest_ref, dest_prev_ref, x_ref, xs_in, xs_out, pk, sem):
    del xs_in
    i = pl.program_id(0)
    tm = x_ref.shape[0]
    half = D_MODEL // 2
    slot = lax.rem(i, 2)
    x = x_ref[...].astype(jnp.float32)
    hi = lax.bitcast_convert_type(x[:, :half], jnp.uint32)
    lo = lax.bitcast_convert_type(x[:, half:], jnp.uint32)
    pk[slot] = hi | (lo >> 16)

    def copy(idx_ref, s, r, j):
        return pltpu.make_async_copy(pk.at[s, pl.ds(r, 1), :], xs_out.at[pl.ds(idx_ref[0, j], 1), :], sem.at[s])

    def start(j, carry):
        copy(dest_ref, slot, lax.rem(j, tm), j).start()
        return carry

    lax.fori_loop(0, TOP_K * tm, start, 0, unroll=8)

    def wait_prev(j, carry):
        copy(dest_prev_ref, 1 - slot, lax.rem(j, tm), j).wait()
        return carry

    @pl.when(i > 0)
    def _():
        lax.fori_loop(0, TOP_K * tm, wait_prev, 0, unroll=8)

    def wait_own(j, carry):
        copy(dest_ref, slot, lax.rem(j, tm), j).wait()
        return carry

    @pl.when(i == pl.num_programs(0) - 1)
    def _():
        lax.fori_loop(0, TOP_K * tm, wait_own, 0, unroll=8)


def _dest_tiles(dest, tm):
    nt = dest.shape[0] // tm
    return dest.reshape(nt, tm, TOP_K).transpose(0, 2, 1).reshape(nt, 1, TOP_K * tm)


def _dispatch_rows(xb, dest_tiles, xs_prev):
    t = xb.shape[0]
    nt, _, two_tm = dest_tiles.shape
    tm = two_tm // TOP_K
    return pl.pallas_call(
        _dispatch_kernel,
        grid=(nt,),
        in_specs=[pl.BlockSpec((None, 1, two_tm), lambda i: (i, 0, 0), memory_space=pltpu.SMEM),
                  pl.BlockSpec((None, 1, two_tm), lambda i: (jnp.maximum(i - 1, 0), 0, 0),
                               memory_space=pltpu.SMEM),
                  pl.BlockSpec((tm, D_MODEL), lambda i: (i, 0)),
                  pl.BlockSpec(memory_space=pl.ANY)],
        out_specs=pl.BlockSpec(memory_space=pl.ANY),
        out_shape=jax.ShapeDtypeStruct(xs_prev.shape, xs_prev.dtype),
        scratch_shapes=[pltpu.VMEM((2, tm, D_MODEL // 2), jnp.uint32),
                        pltpu.SemaphoreType.DMA((2,))],
        input_output_aliases={3: 0},
        compiler_params=_cparams(("arbitrary",), 32),
    )(dest_tiles, dest_tiles, xb, xs_prev)


def _row_copy(src_hbm, idx, buf, slot, r, sem):
    return pltpu.make_async_copy(src_hbm.at[pl.ds(idx, 1), :], buf.at[slot, pl.ds(r, 1), :], sem.at[slot])


def _start_row_gather(src_hbm, idx_ref, buf, slot, sem, n_rows):
    def body(r, carry):
        _row_copy(src_hbm, idx_ref[0, r], buf, slot, r, sem).start()
        return carry
    lax.fori_loop(0, n_rows, body, 0, unroll=8)


def _wait_row_gather(src_hbm, idx_ref, buf, slot, sem, n_rows):
    def body(r, carry):
        _row_copy(src_hbm, idx_ref[0, r], buf, slot, r, sem).wait()
        return carry
    lax.fori_loop(0, n_rows, body, 0, unroll=8)


def _expert_kernel(blk_e_ref, n_used_ref, xs_ref, w1_ref, w3_ref, w2_ref, y_ref, w1_s, w3_s, w2_s):
    i = pl.program_id(0)
    n_used = n_used_ref[0]
    new_expert = jnp.logical_or(i == 0, blk_e_ref[i] != blk_e_ref[jnp.maximum(i - 1, 0)])

    @pl.when(jnp.logical_and(i < n_used, new_expert))
    def _():
        w1_s[...] = w1_ref[...].astype(jnp.bfloat16)
        w3_s[...] = w3_ref[...].astype(jnp.bfloat16)
        w2_s[...] = w2_ref[...].astype(jnp.bfloat16)

    @pl.when(i < n_used)
    def _():
        packed = xs_ref[...]
        x_a = lax.bitcast_convert_type(packed & jnp.uint32(0xFFFF0000), jnp.float32)
        x_b = lax.bitcast_convert_type(packed << 16, jnp.float32)
        xb = jnp.concatenate([x_a, x_b], axis=1).astype(jnp.bfloat16)
        a = _bdot(xb, w1_s[...])
        h = (a * _sigmoid(a)) * _bdot(xb, w3_s[...])
        y_ref[...] = _bdot(h.astype(jnp.bfloat16), w2_s[...])

    @pl.when(i >= n_used)
    def _():
        y_ref[...] = jnp.zeros(y_ref.shape, y_ref.dtype)


def _experts(xs, blk_e, n_used, w1, w3, w2, layer, blk):
    nb = xs.shape[0] // blk

    def used(i, nu):
        return jnp.minimum(i, nu[0] - 1)

    def w_index(i, be, nu):
        return (layer, be[used(i, nu)], 0, 0)

    grid_spec = pltpu.PrefetchScalarGridSpec(
        num_scalar_prefetch=2,
        grid=(nb,),
        in_specs=[pl.BlockSpec((blk, D_MODEL // 2), lambda i, be, nu: (used(i, nu), 0)),
                  pl.BlockSpec((None, None, D_MODEL, D_EXPERT), w_index),
                  pl.BlockSpec((None, None, D_MODEL, D_EXPERT), w_index),
                  pl.BlockSpec((None, None, D_EXPERT, D_MODEL), w_index)],
        out_specs=pl.BlockSpec((blk, D_MODEL), lambda i, be, nu: (i, 0)),
        scratch_shapes=[pltpu.VMEM((D_MODEL, D_EXPERT), jnp.bfloat16),
                        pltpu.VMEM((D_MODEL, D_EXPERT), jnp.bfloat16),
                        pltpu.VMEM((D_EXPERT, D_MODEL), jnp.bfloat16)])
    return pl.pallas_call(
        _expert_kernel,
        grid_spec=grid_spec,
        out_shape=jax.ShapeDtypeStruct((nb * blk, D_MODEL), jnp.float32),
        compiler_params=_cparams(("arbitrary",), 56),
    )(blk_e, n_used, xs, w1, w3, w2)


def _combine_kernel(pos_ref, pos_next_ref, y_hbm, xf_ref, wt_ref, g_ref, b_ref, of_ref, ob_ref, ybuf, sem):
    i = pl.program_id(0)
    tm = xf_ref.shape[0]
    slot = lax.rem(i, 2)

    @pl.when(i == 0)
    def _():
        _start_row_gather(y_hbm, pos_ref, ybuf, 0, sem, TOP_K * tm)

    @pl.when(i + 1 < pl.num_programs(0))
    def _():
        _start_row_gather(y_hbm, pos_next_ref, ybuf, 1 - slot, sem, TOP_K * tm)

    _wait_row_gather(y_hbm, pos_ref, ybuf, slot, sem, TOP_K * tm)
    wt = wt_ref[...]
    moe = wt[:, 0:1] * ybuf[slot, pl.ds(0, tm), :] + wt[:, 1:2] * ybuf[slot, pl.ds(tm, tm), :]
    out = _layer_norm(ALPHA * xf_ref[...] + moe, g_ref[...], b_ref[...])
    of_ref[...] = out
    ob_ref[...] = out.astype(ob_ref.dtype)


def _combine(y_sorted, pos_tiles, xf, wts, ln_g, ln_b, layer):
    t = xf.shape[0]
    nt = pos_tiles.shape[0]
    tm = t // nt
    row_spec = pl.BlockSpec((tm, D_MODEL), lambda i: (i, 0))
    vec_spec = pl.BlockSpec((None, 1, D_MODEL), lambda i: (layer, 0, 0))
    return pl.pallas_call(
        _combine_kernel,
        grid=(nt,),
        in_specs=[pl.BlockSpec((None, 1, TOP_K * tm), lambda i: (i, 0, 0), memory_space=pltpu.SMEM),
                  pl.BlockSpec((None, 1, TOP_K * tm), lambda i: (jnp.minimum(i + 1, nt - 1), 0, 0),
                               memory_space=pltpu.SMEM),
                  pl.BlockSpec(memory_space=pl.ANY),
                  row_spec,
                  pl.BlockSpec((tm, LANES), lambda i: (i, 0)),
                  vec_spec, vec_spec],
        out_specs=[row_spec, row_spec],
        out_shape=[jax.ShapeDtypeStruct((t, D_MODEL), jnp.float32),
                   jax.ShapeDtypeStruct((t, D_MODEL), jnp.bfloat16)],
        scratch_shapes=[pltpu.VMEM((2, TOP_K * tm, D_MODEL), jnp.float32),
                        pltpu.SemaphoreType.DMA((2,))],
        compiler_params=_cparams(("arbitrary",), 48),
    )(pos_tiles, pos_tiles, y_sorted, xf, wts, ln_g, ln_b)


MOE_BLOCK_ROWS = 256
PROJ_TM, PROJ_TN = 1024, 512
MERGE_TM, MERGE_TN = 512, 256
SG_TM = 512
ROUTER_TM = 512
ROW_DMA_TM = 256


def _mixer_layer(xf, xb, p, layer, batch, seq):
    t = batch * seq
    z_ml = _matmul(xb, p["w_ml"], layer, jnp.float32, PROJ_TM, PROJ_TN)
    z_g = _matmul(xb, p["w_mg"], layer, jnp.float32, PROJ_TM, LANES)
    z_at = _matmul(xb, p["w_at"], layer, jnp.bfloat16, PROJ_TM, PROJ_TN)
    z_sp = _matmul(xb, p["w_sp"], layer, jnp.float32, PROJ_TM, PROJ_TN)
    y_ml = _mlstm(z_ml.reshape(batch, seq, ML_W), z_g.reshape(batch, seq, LANES),
                  p["ml_conv_w"], p["ml_gate_b"], p["ml_norm_w"], layer, batch, seq).reshape(t, BRANCH_W)
    y_at = _dilated_attention(z_at.reshape(batch, seq, 3 * ATT_W), batch, seq).reshape(t, BRANCH_W)
    y_sg = _spatial_gating(z_sp, p["sg_ln_g"], p["sg_ln_b"], p["sg_w"], p["sg_b"], layer, SG_TM)
    y_pl = _multiscale_pool(z_sp.reshape(batch, seq, 3 * BRANCH_W), p["pool_w"], p["pool_scale"],
                            layer, batch, seq).reshape(t, BRANCH_W)
    return _merge(xb, xf, (y_ml, y_at, y_sg, y_pl), p["w_gate"], p["b_gate"], p["w_branch"], p["w_out"],
                  p["ln1_g"], p["ln1_b"], layer, MERGE_TM, MERGE_TN)


def _moe_layer(xf, xb, xs_prev, p, layer):
    info, wts, counts = _router(xf, p["w_router"], p["b_router"], layer, ROUTER_TM)
    dest, blk_e, n_used = _dispatch_plan(info, counts, MOE_BLOCK_ROWS, ROUTER_TM)
    dest_tiles = _dest_tiles(dest, ROW_DMA_TM)
    xs = _dispatch_rows(xb, dest_tiles, xs_prev)
    y_sorted = _experts(xs, blk_e, n_used, p["w_exp_gate"], p["w_exp_up"], p["w_exp_down"], layer, MOE_BLOCK_ROWS)
    xf, xb = _combine(y_sorted, dest_tiles, xf, wts, p["ln2_g"], p["ln2_b"], layer)
    return xf, xb, xs


def _prepare_params(w_in, ml_conv_w, ml_gate_b, ml_norm_w, sg_ln_g, sg_ln_b, sg_w, sg_b, pool_w, pool_scale,
                    w_gate, b_gate, w_branch, w_out, ln1_g, ln1_b, w_router_group, b_router_group,
                    w_router_expert, b_router_expert, w_exp_gate, w_exp_up, w_exp_down, ln2_g, ln2_b):
    bf = jnp.bfloat16
    depth = w_in.shape[0]
    c_ml, c_g = ML_W, ML_W + ML_GATES
    c_at = c_g + 3 * ATT_W
    row = lambda a: a.reshape(depth, 1, -1)
    lane_pad = lambda a: jnp.pad(a, [(0, 0)] * (a.ndim - 1) + [(0, LANES - a.shape[-1])])
    w_r = jnp.zeros((depth, D_MODEL, LANES), jnp.float32)
    w_r = w_r.at[:, :, :N_EGROUPS].set(w_router_group).at[:, :, ROUTER_EOFF:ROUTER_EOFF + N_EXPERTS].set(w_router_expert)
    b_r = jnp.zeros((depth, LANES), jnp.float32)
    b_r = b_r.at[:, :N_EGROUPS].set(b_router_group).at[:, ROUTER_EOFF:ROUTER_EOFF + N_EXPERTS].set(b_router_expert)
    w_r_hi = w_r.astype(bf)
    w_r_lo = (w_r - w_r_hi.astype(jnp.float32)).astype(bf)
    sg_b_exp = jnp.repeat(jnp.swapaxes(sg_b, 1, 2), BRANCH_W // SG_GROUPS, axis=2)
    return dict(
        w_ml=w_in[:, :, :c_ml].astype(bf),
        w_mg=lane_pad(w_in[:, :, c_ml:c_g]).astype(bf),
        w_at=w_in[:, :, c_g:c_at].astype(bf),
        w_sp=w_in[:, :, c_at:].astype(bf),
        ml_conv_w=ml_conv_w,
        ml_gate_b=lane_pad(ml_gate_b.reshape(depth, 1, ML_GATES)),
        ml_norm_w=row(ml_norm_w),
        sg_ln_g=row(sg_ln_g), sg_ln_b=row(sg_ln_b), sg_w=sg_w.astype(bf), sg_b=sg_b_exp,
        pool_w=pool_w.astype(bf), pool_scale=row(pool_scale),
        w_gate=w_gate.astype(bf), b_gate=row(b_gate), w_branch=w_branch.astype(bf), w_out=w_out.astype(bf),
        ln1_g=row(ln1_g), ln1_b=row(ln1_b),
        w_router=jnp.stack([w_r_hi, w_r_lo], axis=1), b_router=row(b_r),
        w_exp_gate=w_exp_gate, w_exp_up=w_exp_up, w_exp_down=w_exp_down,
        ln2_g=row(ln2_g), ln2_b=row(ln2_b))


def kernel(x, w_in, ml_conv_w, ml_gate_b, ml_norm_w, sg_ln_g, sg_ln_b, sg_w, sg_b, pool_w, pool_scale, w_gate,
           b_gate, w_branch, w_out, ln1_g, ln1_b, w_router_group, b_router_group, w_router_expert,
           b_router_expert, w_exp_gate, w_exp_up, w_exp_down, ln2_g, ln2_b):
    batch, seq, d = x.shape
    assert d == D_MODEL and seq % (ATT_DIL[-1] * ATT_QT) == 0
    p = _prepare_params(w_in, ml_conv_w, ml_gate_b, ml_norm_w, sg_ln_g, sg_ln_b, sg_w, sg_b, pool_w, pool_scale,
                        w_gate, b_gate, w_branch, w_out, ln1_g, ln1_b, w_router_group, b_router_group,
                        w_router_expert, b_router_expert, w_exp_gate, w_exp_up, w_exp_down, ln2_g, ln2_b)
    t = batch * seq
    xf = x.reshape(t, d)
    xb = xf.astype(jnp.bfloat16)
    n_slots = (t * TOP_K // MOE_BLOCK_ROWS + N_EXPERTS) * MOE_BLOCK_ROWS
    xs = jnp.zeros((n_slots, d // 2), jnp.uint32)
    for layer in range(w_in.shape[0]):
        xf, xb = _mixer_layer(xf, xb, p, layer, batch, seq)
        xf, xb, xs = _moe_layer(xf, xb, xs, p, layer)
    return xf.reshape(batch, seq, d)
```

```python
import functools

import numpy as np
import jax
import jax.numpy as jnp
from jax import lax
from jax.experimental import pallas as pl
from jax.experimental.pallas import tpu as pltpu

D_MODEL = 2048
DEPTH = 4
BRANCH_W = 512
N_BRANCH = 4

ML_HEADS = 4
ML_DQK = 64
ML_DV = 128
ML_CHUNK = 128
ML_QK_W = ML_HEADS * ML_DQK
ML_V_W = ML_HEADS * ML_DV
ML_GATES = 2 * 2 * ML_HEADS
ML_W = 2 * ML_QK_W + 2 * ML_V_W

ATT_WINDOW = (128, 512, 2048)
ATT_DIL = (1, 4, 16)
ATT_NGROUP = 3
ATT_HEADS = 8
ATT_DH = 64
ATT_GW = ATT_HEADS * ATT_DH
ATT_W = ATT_NGROUP * ATT_GW
ATT_NEIGH = 64
ATT_QT = 128

SG_CHUNK = 128
SG_GROUPS = 4
POOL_WINDOWS = (2, 4, 8, 16)
POOL_DG = BRANCH_W // len(POOL_WINDOWS)

N_EGROUPS = 4
EXP_PER_GROUP = 8
N_EXPERTS = N_EGROUPS * EXP_PER_GROUP
TOP_K = 2
D_EXPERT = 512
ROUTER_EOFF = 32

ALPHA = (2.0 * DEPTH) ** 0.25
LN_EPS = 1e-5
NEG = -1e30

ROW_DMA_UNROLL = 8

LANES = 128
VMEM_BYTES_V7X = 64 * 1024 * 1024


def _cparams(semantics, vmem_mb):
    assert vmem_mb * 1024 * 1024 < VMEM_BYTES_V7X
    return pltpu.CompilerParams(dimension_semantics=semantics, vmem_limit_bytes=vmem_mb * 1024 * 1024)


def _sigmoid(x):
    return 1.0 / (1.0 + jnp.exp(-x))


def _bdot(a, b):
    return jnp.dot(a, b, preferred_element_type=jnp.float32)


def _layer_norm(x, g, b):
    mu = jnp.mean(x, axis=-1, keepdims=True)
    xc = x - mu
    var = jnp.mean(xc * xc, axis=-1, keepdims=True)
    return xc * lax.rsqrt(var + LN_EPS) * g + b


def _mm_kernel(x_ref, w_ref, o_ref):
    o_ref[...] = _bdot(x_ref[...], w_ref[...]).astype(o_ref.dtype)


def _matmul(x, w, layer, out_dtype, tm, tn):
    t, k = x.shape
    n = w.shape[2]
    return pl.pallas_call(
        _mm_kernel,
        grid=(t // tm, n // tn),
        in_specs=[pl.BlockSpec((tm, k), lambda i, j: (i, 0)),
                  pl.BlockSpec((None, k, tn), lambda i, j: (layer, 0, j))],
        out_specs=pl.BlockSpec((tm, tn), lambda i, j: (i, j)),
        out_shape=jax.ShapeDtypeStruct((t, n), out_dtype),
        compiler_params=_cparams(("parallel", "arbitrary"), 40),
    )(x, w)


def _split_hi_lo(x):
    hi = x.astype(jnp.bfloat16)
    lo = (x - hi.astype(jnp.float32)).astype(jnp.bfloat16)
    return hi, lo


def _mlstm_kernel(q_ref, k_ref, v_ref, o_ref, g_ref, cw_ref, gb_ref, nw_ref, y_ref,
                  qc_s, kc_s, gp_s, hf_s, hb_s, ct_s):
    s_len = q_ref.shape[0]
    n_chunks = s_len // ML_CHUNK
    lc = ML_CHUNK

    row_c = lax.broadcasted_iota(jnp.int32, (lc, 1), 0)
    cw = cw_ref[...]
    gate_lane = lax.broadcasted_iota(jnp.int32, (lc, LANES), 1)

    def prep(c, carry):
        r0 = pl.multiple_of(c * lc, lc)
        rows = pl.ds(r0, lc)

        def conv_silu(ref, w):
            x = ref[rows, :]
            prev_row = jnp.where(c > 0, ref[pl.ds(jnp.maximum(r0 - 1, 0), 1), :], 0.0)
            next_row = jnp.where(c < n_chunks - 1, ref[pl.ds(jnp.minimum(r0 + lc, s_len - 1), 1), :], 0.0)
            x_prev = jnp.where(row_c == 0, prev_row, pltpu.roll(x, 1, 0))
            x_next = jnp.where(row_c == lc - 1, next_row, pltpu.roll(x, lc - 1, 0))
            y = x_prev * w[0:1] + x * w[1:2] + x_next * w[2:3]
            return y * _sigmoid(y)

        qc_s[rows, :] = (conv_silu(q_ref, cw[:, :ML_QK_W]) * (ML_DQK ** -0.5)).astype(jnp.bfloat16)
        kc_s[rows, :] = conv_silu(k_ref, cw[:, ML_QK_W:]).astype(jnp.bfloat16)

        gp = g_ref[rows, :] + gb_ref[...]
        log_sig = jnp.minimum(gp, 0.0) - jnp.log(1.0 + jnp.exp(-jnp.abs(gp)))
        gp_s[rows, :] = jnp.where((gate_lane % 8) >= ML_HEADS, log_sig, gp)
        return carry

    lax.fori_loop(0, n_chunks, prep, 0)

    r_i = lax.broadcasted_iota(jnp.int32, (lc, lc), 0)
    c_i = lax.broadcasted_iota(jnp.int32, (lc, lc), 1)
    lower = c_i <= r_i
    upper = c_i >= r_i
    tril = jnp.where(lower, 1.0, 0.0).astype(jnp.bfloat16)
    triu = jnp.where(upper, 1.0, 0.0).astype(jnp.bfloat16)
    ones_ext = jnp.ones((lc, ML_DV), jnp.bfloat16)
    head_of_lane = lax.broadcasted_iota(jnp.int32, (lc, ML_QK_W), 1) // ML_DQK

    ct_s[...] = jnp.zeros(ct_s.shape, ct_s.dtype)

    def chunk_dir(c, direction, ms, h_out):
        r0 = pl.multiple_of(c * lc, lc)
        rows = pl.ds(r0, lc)
        gc = gp_s[rows, :]
        gct = gc.T
        gc_hi, gc_lo = _split_hi_lo(gc)
        gct_hi, gct_lo = _split_hi_lo(gct)
        if direction == 0:
            cum_col = _bdot(tril, gc_hi) + _bdot(tril, gc_lo)
            cum_row = _bdot(gct_hi, triu) + _bdot(gct_lo, triu)
            mask = lower
        else:
            cum_col = _bdot(triu, gc_hi) + _bdot(triu, gc_lo)
            cum_row = _bdot(gct_hi, tril) + _bdot(gct_lo, tril)
            mask = upper
        qch = qc_s[rows, :]
        kch = kc_s[rows, :]
        vch = v_ref[rows, :].astype(jnp.bfloat16)
        kf = kch.astype(jnp.float32)

        new_ms = []
        wk_cols = []
        decays = []
        for h in range(ML_HEADS):
            ci = direction * 8 + h
            cf = ci + ML_HEADS
            m = ms[h]
            b_col = cum_col[:, cf:cf + 1]
            b_row = cum_row[cf:cf + 1, :]
            i_row = gct[ci:ci + 1, :]
            i_col = gc[:, ci:ci + 1]
            total = b_row[:, lc - 1:lc] if direction == 0 else b_row[:, 0:1]
            dmat = jnp.where(mask, b_col - b_row + i_row, NEG)
            m_inter = m + b_col
            m_t = jnp.maximum(m_inter, jnp.max(dmat, axis=1, keepdims=True))
            w_inter = jnp.exp(m_inter - m_t)
            qh = qch[:, h * ML_DQK:(h + 1) * ML_DQK]
            kh = kch[:, h * ML_DQK:(h + 1) * ML_DQK]
            v_ext = jnp.concatenate([vch[:, h * ML_DV:(h + 1) * ML_DV], ones_ext], axis=1)
            s = lax.dot_general(qh, kh, (((1,), (1,)), ((), ())),
                                preferred_element_type=jnp.float32) * jnp.exp(dmat - m_t)
            ct = ct_s[direction * ML_HEADS + h]
            both = w_inter * _bdot(qh, ct.astype(jnp.bfloat16)) + _bdot(s.astype(jnp.bfloat16), v_ext)
            num = both[:, :ML_DV]
            den = both[:, ML_DV:]
            h_out[rows, h * ML_DV:(h + 1) * ML_DV] = num / jnp.maximum(jnp.abs(den), jnp.exp(-m_t))

            g_row = total - b_row + i_row
            m_new = jnp.maximum(m + total, jnp.max(g_row, axis=1, keepdims=True))
            decays.append(jnp.exp(m + total - m_new))
            wk_cols.append(jnp.exp(total - b_col + i_col - m_new))
            new_ms.append(m_new)

        wk_all = wk_cols[0]
        for h in range(1, ML_HEADS):
            wk_all = jnp.where(head_of_lane >= h, wk_cols[h], wk_all)
        kwt = (kf * wk_all).T.astype(jnp.bfloat16)
        for h in range(ML_HEADS):
            v_ext = jnp.concatenate([vch[:, h * ML_DV:(h + 1) * ML_DV], ones_ext], axis=1)
            idx = direction * ML_HEADS + h
            ct_s[idx] = decays[h] * ct_s[idx] + _bdot(kwt[h * ML_DQK:(h + 1) * ML_DQK, :], v_ext)
        return new_ms

    def body(c, carry):
        ms_f = chunk_dir(c, 0, list(carry[:ML_HEADS]), hf_s)
        ms_b = chunk_dir(n_chunks - 1 - c, 1, list(carry[ML_HEADS:]), hb_s)
        return tuple(ms_f) + tuple(ms_b)

    zero = jnp.zeros((1, 1), jnp.float32)
    lax.fori_loop(0, n_chunks, body, (zero,) * (2 * ML_HEADS))

    nw = nw_ref[...]

    def finish(c, carry):
        rows = pl.ds(pl.multiple_of(c * lc, lc), lc)
        hsum = hf_s[rows, :] + hb_s[rows, :]
        og = _sigmoid(o_ref[rows, :])
        for h in range(ML_HEADS):
            sl = slice(h * ML_DV, (h + 1) * ML_DV)
            hh = hsum[:, sl]
            mu = jnp.mean(hh, axis=-1, keepdims=True)
            hc = hh - mu
            var = jnp.mean(hc * hc, axis=-1, keepdims=True)
            y_ref[rows, sl] = (hc * lax.rsqrt(var + LN_EPS) * nw[:, sl] * og[:, sl]).astype(y_ref.dtype)
        return carry

    lax.fori_loop(0, n_chunks, finish, 0)


def _mlstm(z_ml, z_g, conv_w, gate_b, norm_w, layer, batch, seq):
    return pl.pallas_call(
        _mlstm_kernel,
        grid=(batch,),
        in_specs=[pl.BlockSpec((None, seq, ML_QK_W), lambda b: (b, 0, 0)),
                  pl.BlockSpec((None, seq, ML_QK_W), lambda b: (b, 0, 1)),
                  pl.BlockSpec((None, seq, ML_V_W), lambda b: (b, 0, 1)),
                  pl.BlockSpec((None, seq, ML_V_W), lambda b: (b, 0, 2)),
                  pl.BlockSpec((None, seq, LANES), lambda b: (b, 0, 0)),
                  pl.BlockSpec((None, 3, 2 * ML_QK_W), lambda b: (layer, 0, 0)),
                  pl.BlockSpec((None, 1, LANES), lambda b: (layer, 0, 0)),
                  pl.BlockSpec((None, 1, ML_V_W), lambda b: (layer, 0, 0))],
        out_specs=pl.BlockSpec((None, seq, ML_V_W), lambda b: (b, 0, 0)),
        out_shape=jax.ShapeDtypeStruct((batch, seq, ML_V_W), jnp.bfloat16),
        scratch_shapes=[pltpu.VMEM((seq, ML_QK_W), jnp.bfloat16),
                        pltpu.VMEM((seq, ML_QK_W), jnp.bfloat16),
                        pltpu.VMEM((seq, LANES), jnp.float32),
                        pltpu.VMEM((seq, ML_V_W), jnp.float32),
                        pltpu.VMEM((seq, ML_V_W), jnp.float32),
                        pltpu.VMEM((2 * ML_HEADS, ML_DQK, 2 * ML_DV), jnp.float32)],
        compiler_params=_cparams(("parallel",), 56),
    )(z_ml, z_ml, z_ml, z_ml, z_g, conv_w, gate_b, norm_w)


def _alibi_slopes():
    n = ATT_NGROUP * ATT_HEADS
    s = 2.0 ** (-8.0 * np.arange(1, n + 1) / n)
    return s.reshape(ATT_NGROUP, ATT_HEADS).astype(np.float32)


ATT_SLABS = ATT_GW // LANES
ATT_STAGE_ROWS = 256


def _attn_group(q_ref, k_ref, v_ref, y_ref, stage_s, qs, ks, vs, oacc, lacc, *, group, seq, first, last):
    dil = ATT_DIL[group]
    assert ATT_WINDOW[group] // (2 * dil) == ATT_NEIGH and (not last or dil == 1)
    sub_len = seq // dil
    slopes = [float(s) for s in _alibi_slopes()[group] * np.float32(dil)]
    kw = min(2 * ATT_QT, sub_len)
    tiles_per_sub = sub_len // ATT_QT

    if dil == 1:
        q_src, k_src, v_src = q_ref, k_ref, v_ref
    else:
        for src, dst in ((q_ref, qs), (k_ref, ks), (v_ref, vs)):
            def stage(c, carry, src=src):
                rows = pl.ds(pl.multiple_of(c * ATT_STAGE_ROWS, ATT_STAGE_ROWS), ATT_STAGE_ROWS)
                x = src[rows, :].astype(jnp.float32)
                for sl in range(ATT_SLABS):
                    stage_s[sl, rows, :] = x[:, sl * LANES:(sl + 1) * LANES]
                return carry

            lax.fori_loop(0, seq // ATT_STAGE_ROWS, stage, 0)

            def gather_residue(r, carry, dst=dst):
                out_rows = pl.ds(pl.multiple_of(r * sub_len, sub_len), sub_len)
                for sl in range(ATT_SLABS):
                    x = stage_s[sl, pl.ds(r, sub_len, stride=dil), :]
                    dst[out_rows, sl * LANES:(sl + 1) * LANES] = x.astype(jnp.bfloat16)
                return carry

            lax.fori_loop(0, dil, gather_residue, 0)
        q_src, k_src, v_src = qs, ks, vs

    def tile(idx, carry):
        r = idx // tiles_per_sub
        q0 = (idx % tiles_per_sub) * ATT_QT
        k0 = jnp.clip(q0 - ATT_NEIGH, 0, sub_len - kw)
        base = r * sub_len
        qpos = q0 + lax.broadcasted_iota(jnp.int32, (ATT_QT, 1), 0)
        kpos = k0 + lax.broadcasted_iota(jnp.int32, (1, kw), 1)
        arel = jnp.abs(kpos - qpos).astype(jnp.float32)
        valid = arel <= float(ATT_NEIGH)
        q = q_src[pl.ds(pl.multiple_of(base + q0, ATT_QT), ATT_QT), :] * (ATT_DH ** -0.5)
        k = k_src[pl.ds(pl.multiple_of(base + k0, ATT_NEIGH), kw), :]
        v = v_src[pl.ds(pl.multiple_of(base + k0, ATT_NEIGH), kw), :]
        o_parts, l_parts = [], []
        for h in range(ATT_HEADS):
            sl = slice(h * ATT_DH, (h + 1) * ATT_DH)
            s = lax.dot_general(q[:, sl], k[:, sl], (((1,), (1,)), ((), ())),
                                preferred_element_type=jnp.float32)
            s = jnp.where(valid, s - slopes[h] * arel, NEG)
            m = jnp.max(s, axis=1, keepdims=True)
            p = jnp.exp(s - m)
            den = jnp.sum(p, axis=1, keepdims=True)
            o_parts.append(_bdot(p.astype(jnp.bfloat16), v[:, sl]) / den)
            l_parts.append(jnp.broadcast_to(m + jnp.log(den), (ATT_QT, ATT_DH)))
        o_new = jnp.concatenate(o_parts, axis=1)
        l_new = jnp.concatenate(l_parts, axis=1)

        start = q0 * dil + r
        nat_rows = pl.ds(start, ATT_QT) if dil == 1 else pl.ds(start, ATT_QT, stride=dil)
        for sl in range(ATT_SLABS):
            cols = slice(sl * LANES, (sl + 1) * LANES)
            o_sl, l_sl = o_new[:, cols], l_new[:, cols]
            if not first:
                o_old, l_old = oacc[sl, nat_rows, :], lacc[sl, nat_rows, :]
                mx = jnp.maximum(l_old, l_sl)
                w_old, w_new = jnp.exp(l_old - mx), jnp.exp(l_sl - mx)
                tot = w_old + w_new
                o_sl = (w_old * o_old + w_new * o_sl) / tot
                l_sl = mx + jnp.log(tot)
            if last:
                y_ref[pl.ds(pl.multiple_of(start, ATT_QT), ATT_QT), cols] = o_sl.astype(y_ref.dtype)
            else:
                oacc[sl, nat_rows, :] = o_sl
                lacc[sl, nat_rows, :] = l_sl
        return carry

    lax.fori_loop(0, seq // ATT_QT, tile, 0)


def _attn_kernel(q_ref, k_ref, v_ref, y_ref, stage_s, qs, ks, vs, oacc, lacc, *, seq):
    gi = pl.program_id(1)
    scratch = (stage_s, qs, ks, vs, oacc, lacc)
    for step in range(ATT_NGROUP):
        @pl.when(gi == step)
        def _(step=step):
            _attn_group(q_ref, k_ref, v_ref, y_ref, *scratch, group=ATT_NGROUP - 1 - step, seq=seq,
                        first=step == 0, last=step == ATT_NGROUP - 1)


def _dilated_attention(z_at, batch, seq):
    def in_spec(which):
        return pl.BlockSpec((None, seq, ATT_GW),
                            lambda b, gi: (b, 0, which * ATT_NGROUP + ATT_NGROUP - 1 - gi))

    slab = pltpu.VMEM((ATT_SLABS, seq, LANES), jnp.float32)
    sub = pltpu.VMEM((seq, ATT_GW), jnp.bfloat16)
    return pl.pallas_call(
        functools.partial(_attn_kernel, seq=seq),
        grid=(batch, ATT_NGROUP),
        in_specs=[in_spec(0), in_spec(1), in_spec(2)],
        out_specs=pl.BlockSpec((None, seq, ATT_GW), lambda b, gi: (b, 0, 0)),
        out_shape=jax.ShapeDtypeStruct((batch, seq, ATT_GW), jnp.bfloat16),
        scratch_shapes=[slab, sub, sub, sub, slab, slab],
        compiler_params=_cparams(("parallel", "arbitrary"), 48),
    )(z_at, z_at, z_at)


def _gelu_tanh(x):
    return x * (0.5 * (1.0 + jnp.tanh(np.sqrt(2.0 / np.pi).astype(np.float32) * (x + 0.044715 * (x * x * x)))))


def _sg_kernel(u_ref, v_ref, g_ref, b_ref, ws_ref, bs_ref, y_ref):
    tm = u_ref.shape[0]
    dg = BRANCH_W // SG_GROUPS
    u = _gelu_tanh(u_ref[...])
    vn = _layer_norm(_gelu_tanh(v_ref[...]), g_ref[...], b_ref[...]).astype(jnp.bfloat16)
    bs = bs_ref[...]
    for c in range(tm // SG_CHUNK):
        rows = slice(c * SG_CHUNK, (c + 1) * SG_CHUNK)
        for g in range(SG_GROUPS):
            cols = slice(g * dg, (g + 1) * dg)
            mixed = _bdot(ws_ref[g], vn[rows, cols]) + bs[:, cols]
            y_ref[rows, cols] = (u[rows, cols] * mixed).astype(y_ref.dtype)


def _spatial_gating(z_sp, ln_g, ln_b, w_s, b_s_exp, layer, tm):
    t = z_sp.shape[0]
    return pl.pallas_call(
        _sg_kernel,
        grid=(t // tm,),
        in_specs=[pl.BlockSpec((tm, BRANCH_W), lambda i: (i, 0)),
                  pl.BlockSpec((tm, BRANCH_W), lambda i: (i, 1)),
                  pl.BlockSpec((None, 1, BRANCH_W), lambda i: (layer, 0, 0)),
                  pl.BlockSpec((None, 1, BRANCH_W), lambda i: (layer, 0, 0)),
                  pl.BlockSpec((None, SG_GROUPS, SG_CHUNK, SG_CHUNK), lambda i: (layer, 0, 0, 0)),
                  pl.BlockSpec((None, SG_CHUNK, BRANCH_W), lambda i: (layer, 0, 0))],
        out_specs=pl.BlockSpec((tm, BRANCH_W), lambda i: (i, 0)),
        out_shape=jax.ShapeDtypeStruct((t, BRANCH_W), jnp.bfloat16),
        compiler_params=_cparams(("parallel",), 32),
    )(z_sp, z_sp, ln_g, ln_b, w_s, b_s_exp)


def _pool_kernel(p_ref, w_ref, sc_ref, y_ref):
    s_len = p_ref.shape[0]
    row = lax.broadcasted_iota(jnp.int32, (s_len, 1), 0)
    sc = sc_ref[...]
    for g, win in enumerate(POOL_WINDOWS):
        cols = slice(g * POOL_DG, (g + 1) * POOL_DG)
        half = win // 2
        p = p_ref[:, cols]
        acc = jnp.zeros_like(p)
        for j in range(-half, half):
            if j == 0:
                acc = acc + p
            else:
                shifted = pltpu.roll(p, (-j) % s_len, 0)
                ok = (row + j >= 0) & (row + j < s_len)
                acc = acc + jnp.where(ok, shifted, 0.0)
        cnt = (jnp.minimum(row + half, s_len) - jnp.maximum(row - half, 0)).astype(jnp.float32)
        d = acc / cnt - p
        y = _bdot(d.astype(jnp.bfloat16), w_ref[g]) * sc[:, cols]
        y_ref[:, cols] = y.astype(y_ref.dtype)


def _multiscale_pool(z_sp, pool_w, pool_scale, layer, batch, seq):
    return pl.pallas_call(
        _pool_kernel,
        grid=(batch,),
        in_specs=[pl.BlockSpec((None, seq, BRANCH_W), lambda b: (b, 0, 2)),
                  pl.BlockSpec((None, len(POOL_WINDOWS), POOL_DG, POOL_DG), lambda b: (layer, 0, 0, 0)),
                  pl.BlockSpec((None, 1, BRANCH_W), lambda b: (layer, 0, 0))],
        out_specs=pl.BlockSpec((None, seq, BRANCH_W), lambda b: (b, 0, 0)),
        out_shape=jax.ShapeDtypeStruct((batch, seq, BRANCH_W), jnp.bfloat16),
        compiler_params=_cparams(("parallel",), 48),
    )(z_sp, pool_w, pool_scale)


def _gate_mix_kernel(xb_ref, y0, y1, y2, y3, wg0, wg1, wg2, wg3, bg0, bg1, bg2, bg3, wb_ref, o_ref):
    xb = xb_ref[...]
    merged = None
    for n, (y, wg, bg) in enumerate(((y0, wg0, bg0), (y1, wg1, bg1), (y2, wg2, bg2), (y3, wg3, bg3))):
        gate = _sigmoid(_bdot(xb, wg[...]) + bg[...])
        term = gate * _bdot(y[...], wb_ref[n])
        merged = term if merged is None else merged + term
    o_ref[...] = merged.astype(o_ref.dtype)


def _gate_mix(xb, ys, w_gate, b_gate, w_branch, layer, tm, tn):
    t = xb.shape[0]
    nj = D_MODEL // tn
    row_spec = lambda w: pl.BlockSpec((tm, w), lambda i, j: (i, 0))
    wg_specs = [pl.BlockSpec((None, D_MODEL, tn), lambda i, j, n=n: (layer, 0, n * nj + j)) for n in range(N_BRANCH)]
    bg_specs = [pl.BlockSpec((None, 1, tn), lambda i, j, n=n: (layer, 0, n * nj + j)) for n in range(N_BRANCH)]
    return pl.pallas_call(
        _gate_mix_kernel,
        grid=(t // tm, nj),
        in_specs=[row_spec(D_MODEL)] + [row_spec(BRANCH_W)] * N_BRANCH + wg_specs + bg_specs
                 + [pl.BlockSpec((None, N_BRANCH, BRANCH_W, tn), lambda i, j: (layer, 0, 0, j))],
        out_specs=pl.BlockSpec((tm, tn), lambda i, j: (i, j)),
        out_shape=jax.ShapeDtypeStruct((t, D_MODEL), jnp.bfloat16),
        compiler_params=_cparams(("parallel", "arbitrary"), 48),
    )(xb, *ys, *([w_gate] * N_BRANCH), *([b_gate] * N_BRANCH), w_branch)


def _out_proj_ln_kernel(m_ref, wo_ref, xf_ref, g_ref, b_ref, of_ref, ob_ref):
    out = _layer_norm(ALPHA * xf_ref[...] + _bdot(m_ref[...], wo_ref[...]), g_ref[...], b_ref[...])
    of_ref[...] = out
    ob_ref[...] = out.astype(ob_ref.dtype)


def _out_proj_ln(merged, w_out, xf, ln_g, ln_b, layer, tm):
    t = xf.shape[0]
    row_spec = pl.BlockSpec((tm, D_MODEL), lambda i: (i, 0))
    vec_spec = pl.BlockSpec((None, 1, D_MODEL), lambda i: (layer, 0, 0))
    return pl.pallas_call(
        _out_proj_ln_kernel,
        grid=(t // tm,),
        in_specs=[row_spec, pl.BlockSpec((None, D_MODEL, D_MODEL), lambda i: (layer, 0, 0)),
                  row_spec, vec_spec, vec_spec],
        out_specs=[row_spec, row_spec],
        out_shape=[jax.ShapeDtypeStruct((t, D_MODEL), jnp.float32),
                   jax.ShapeDtypeStruct((t, D_MODEL), jnp.bfloat16)],
        compiler_params=_cparams(("parallel",), 48),
    )(merged, w_out, xf, ln_g, ln_b)


def _router_kernel(x_ref, w_ref, b_ref, id_ref, wt_ref, cnt_ref):
    x_hi, x_lo = _split_hi_lo(x_ref[...])
    w_hi, w_lo = w_ref[0], w_ref[1]
    logits = _bdot(x_hi, w_hi) + _bdot(x_lo, w_hi) + _bdot(x_hi, w_lo) + b_ref[...]
    lane = lax.broadcasted_iota(jnp.int32, logits.shape, 1)
    lane_f = lane.astype(jnp.float32)
    no_lane = float(LANES)

    def first_argmax(vals):
        mx = jnp.max(vals, axis=1, keepdims=True)
        idx = jnp.min(jnp.where(vals == mx, lane_f, no_lane), axis=1, keepdims=True)
        return mx, idx.astype(jnp.int32)

    is_group = lane < N_EGROUPS
    g_max, g_sel = first_argmax(jnp.where(is_group, logits, NEG))
    p_sel = 1.0 / jnp.sum(jnp.where(is_group, jnp.exp(logits - g_max), 0.0), axis=1, keepdims=True)

    lo = ROUTER_EOFF + g_sel * EXP_PER_GROUP
    cand = jnp.where((lane >= lo) & (lane < lo + EXP_PER_GROUP), logits, NEG)
    l1, i1 = first_argmax(cand)
    l2, i2 = first_argmax(jnp.where(lane == i1, NEG, cand))
    e = jnp.exp(l2 - l1)
    w1 = p_sel / (1.0 + e)
    w2 = p_sel * e / (1.0 + e)
    e1, e2 = i1 - ROUTER_EOFF, i2 - ROUTER_EOFF

    tm = logits.shape[0]
    hit1, hit2 = lane == e1, lane == e2
    hits = jnp.where(jnp.logical_or(hit1, hit2), 1.0, 0.0)
    earlier = (lax.broadcasted_iota(jnp.int32, (tm, tm), 1) < lax.broadcasted_iota(jnp.int32, (tm, tm), 0))
    before = _bdot(jnp.where(earlier, 1.0, 0.0).astype(jnp.bfloat16), hits.astype(jnp.bfloat16))
    rank1 = jnp.sum(jnp.where(hit1, before, 0.0), axis=1, keepdims=True).astype(jnp.int32)
    rank2 = jnp.sum(jnp.where(hit2, before, 0.0), axis=1, keepdims=True).astype(jnp.int32)
    counts = jnp.sum(hits, axis=0, keepdims=True).astype(jnp.int32)

    id_ref[...] = jnp.where(lane == 0, e1, jnp.where(lane == 1, e2,
                            jnp.where(lane == 2, rank1, jnp.where(lane == 3, rank2, 0))))
    wt_ref[...] = jnp.where(lane == 0, w1, jnp.where(lane == 1, w2, 0.0))
    cnt_ref[...] = jnp.broadcast_to(counts, cnt_ref.shape)


def _router(xf, w_r, b_r, layer, tm):
    t = xf.shape[0]
    out_spec = pl.BlockSpec((tm, LANES), lambda i: (i, 0))
    return pl.pallas_call(
        _router_kernel,
        grid=(t // tm,),
        in_specs=[pl.BlockSpec((tm, D_MODEL), lambda i: (i, 0)),
                  pl.BlockSpec((None, 2, D_MODEL, LANES), lambda i: (layer, 0, 0, 0)),
                  pl.BlockSpec((None, 1, LANES), lambda i: (layer, 0, 0))],
        out_specs=[out_spec, out_spec, pl.BlockSpec((None, 8, LANES), lambda i: (i, 0, 0))],
        out_shape=[jax.ShapeDtypeStruct((t, LANES), jnp.int32),
                   jax.ShapeDtypeStruct((t, LANES), jnp.float32),
                   jax.ShapeDtypeStruct((t // tm, 8, LANES), jnp.int32)],
        compiler_params=_cparams(("parallel",), 32),
    )(xf, w_r, b_r)


def _dispatch_plan(info, counts, blk, tm):
    t = info.shape[0]
    c = counts[:, 0, :N_EXPERTS]
    tile_off = jnp.cumsum(c, axis=0) - c
    total = jnp.sum(c, axis=0)
    padded = (total + blk - 1) // blk * blk
    pad_end = jnp.cumsum(padded)
    base = (pad_end - padded)[None, :] + tile_off
    base_tok = jnp.repeat(base, tm, axis=0)
    hit = info[:, :TOP_K, None] == jnp.arange(N_EXPERTS, dtype=jnp.int32)[None, None, :]
    dest = jnp.sum(jnp.where(hit, base_tok[:, None, :], 0), axis=2) + info[:, TOP_K:2 * TOP_K]
    nb = t * TOP_K // blk + N_EXPERTS
    blk_start = jnp.arange(nb, dtype=jnp.int32) * blk
    blk_e = jnp.minimum(jnp.sum((pad_end[None, :] <= blk_start[:, None]).astype(jnp.int32), axis=1),
                        N_EXPERTS - 1)
    n_used = (pad_end[-1] // blk).astype(jnp.int32).reshape(1)
    return dest.astype(jnp.int32), blk_e, n_used


def _dispatch_kernel(dest_ref, dest_prev_ref, x_ref, xs_in, xs_out, pk, sem):
    del xs_in
    i = pl.program_id(0)
    tm = x_ref.shape[0]
    half = D_MODEL // 2
    slot = lax.rem(i, 2)
    x = x_ref[...].astype(jnp.float32)
    hi = lax.bitcast_convert_type(x[:, :half], jnp.uint32)
    lo = lax.bitcast_convert_type(x[:, half:], jnp.uint32)
    pk[slot] = hi | (lo >> 16)

    def copy(idx_ref, s, r, j):
        return pltpu.make_async_copy(pk.at[s, pl.ds(r, 1), :], xs_out.at[pl.ds(idx_ref[0, j], 1), :], sem.at[s])

    def start(g, carry):
        for u in range(ROW_DMA_UNROLL):
            j = g * ROW_DMA_UNROLL + u
            copy(dest_ref, slot, lax.rem(j, tm), j).start(priority=u % 2)
        return carry

    lax.fori_loop(0, TOP_K * tm // ROW_DMA_UNROLL, start, 0)

    def wait_prev(j, carry):
        copy(dest_prev_ref, 1 - slot, lax.rem(j, tm), j).wait()
        return carry

    @pl.when(i > 0)
    def _():
        lax.fori_loop(0, TOP_K * tm, wait_prev, 0, unroll=8)

    def wait_own(j, carry):
        copy(dest_ref, slot, lax.rem(j, tm), j).wait()
        return carry

    @pl.when(i == pl.num_programs(0) - 1)
    def _():
        lax.fori_loop(0, TOP_K * tm, wait_own, 0, unroll=8)


def _dest_tiles(dest, tm):
    nt = dest.shape[0] // tm
    return dest.reshape(nt, tm, TOP_K).transpose(0, 2, 1).reshape(nt, 1, TOP_K * tm)


def _dispatch_rows(xb, dest_tiles, xs_prev):
    t = xb.shape[0]
    nt, _, two_tm = dest_tiles.shape
    tm = two_tm // TOP_K
    return pl.pallas_call(
        _dispatch_kernel,
        grid=(nt,),
        in_specs=[pl.BlockSpec((None, 1, two_tm), lambda i: (i, 0, 0), memory_space=pltpu.SMEM),
                  pl.BlockSpec((None, 1, two_tm), lambda i: (jnp.maximum(i - 1, 0), 0, 0),
                               memory_space=pltpu.SMEM),
                  pl.BlockSpec((tm, D_MODEL), lambda i: (i, 0)),
                  pl.BlockSpec(memory_space=pl.ANY)],
        out_specs=pl.BlockSpec(memory_space=pl.ANY),
        out_shape=jax.ShapeDtypeStruct(xs_prev.shape, xs_prev.dtype),
        scratch_shapes=[pltpu.VMEM((2, tm, D_MODEL // 2), jnp.uint32),
                        pltpu.SemaphoreType.DMA((2,))],
        input_output_aliases={3: 0},
        compiler_params=_cparams(("arbitrary",), 32),
    )(dest_tiles, dest_tiles, xb, xs_prev)


def _row_copy(src_hbm, idx, buf, slot, r, sem):
    return pltpu.make_async_copy(src_hbm.at[pl.ds(idx, 1), :], buf.at[slot, pl.ds(r, 1), :], sem.at[slot])


def _start_row_gather(src_hbm, idx_ref, buf, slot, sem, n_rows):
    def body(g, carry):
        for u in range(ROW_DMA_UNROLL):
            r = g * ROW_DMA_UNROLL + u
            _row_copy(src_hbm, idx_ref[0, r], buf, slot, r, sem).start(priority=u % 2)
        return carry
    lax.fori_loop(0, n_rows // ROW_DMA_UNROLL, body, 0)


def _wait_row_gather(src_hbm, idx_ref, buf, slot, sem, n_rows):
    def body(r, carry):
        _row_copy(src_hbm, idx_ref[0, r], buf, slot, r, sem).wait()
        return carry
    lax.fori_loop(0, n_rows, body, 0, unroll=8)


def _expert_kernel(blk_e_ref, n_used_ref, xs_ref, w1_ref, w3_ref, w2_ref, y_ref, w1_s, w3_s, w2_s):
    i = pl.program_id(0)
    n_used = n_used_ref[0]
    new_expert = jnp.logical_or(i == 0, blk_e_ref[i] != blk_e_ref[jnp.maximum(i - 1, 0)])

    @pl.when(jnp.logical_and(i < n_used, new_expert))
    def _():
        w1_s[...] = w1_ref[...].astype(jnp.bfloat16)
        w3_s[...] = w3_ref[...].astype(jnp.bfloat16)
        w2_s[...] = w2_ref[...].astype(jnp.bfloat16)

    @pl.when(i < n_used)
    def _():
        packed = xs_ref[...]
        x_a = lax.bitcast_convert_type(packed & jnp.uint32(0xFFFF0000), jnp.float32)
        x_b = lax.bitcast_convert_type(packed << 16, jnp.float32)
        xb = jnp.concatenate([x_a, x_b], axis=1).astype(jnp.bfloat16)
        a = _bdot(xb, w1_s[...])
        h = (a * _sigmoid(a)) * _bdot(xb, w3_s[...])
        y_ref[...] = _bdot(h.astype(jnp.bfloat16), w2_s[...])

    @pl.when(i >= n_used)
    def _():
        y_ref[...] = jnp.zeros(y_ref.shape, y_ref.dtype)


def _experts(xs, blk_e, n_used, w1, w3, w2, layer, blk):
    nb = xs.shape[0] // blk

    def used(i, nu):
        return jnp.minimum(i, nu[0] - 1)

    def w_index(i, be, nu):
        return (layer, be[used(i, nu)], 0, 0)

    grid_spec = pltpu.PrefetchScalarGridSpec(
        num_scalar_prefetch=2,
        grid=(nb,),
        in_specs=[pl.BlockSpec((blk, D_MODEL // 2), lambda i, be, nu: (used(i, nu), 0)),
                  pl.BlockSpec((None, None, D_MODEL, D_EXPERT), w_index),
                  pl.BlockSpec((None, None, D_MODEL, D_EXPERT), w_index),
                  pl.BlockSpec((None, None, D_EXPERT, D_MODEL), w_index)],
        out_specs=pl.BlockSpec((blk, D_MODEL), lambda i, be, nu: (i, 0)),
        scratch_shapes=[pltpu.VMEM((D_MODEL, D_EXPERT), jnp.bfloat16),
                        pltpu.VMEM((D_MODEL, D_EXPERT), jnp.bfloat16),
                        pltpu.VMEM((D_EXPERT, D_MODEL), jnp.bfloat16)])
    return pl.pallas_call(
        _expert_kernel,
        grid_spec=grid_spec,
        out_shape=jax.ShapeDtypeStruct((nb * blk, D_MODEL), jnp.float32),
        compiler_params=_cparams(("arbitrary",), 56),
    )(blk_e, n_used, xs, w1, w3, w2)


def _combine_kernel(pos_ref, pos_next_ref, y_hbm, xf_ref, wt_ref, g_ref, b_ref, of_ref, ob_ref, ybuf, sem):
    i = pl.program_id(0)
    tm = xf_ref.shape[0]
    slot = lax.rem(i, 2)

    @pl.when(i == 0)
    def _():
        _start_row_gather(y_hbm, pos_ref, ybuf, 0, sem, TOP_K * tm)

    @pl.when(i + 1 < pl.num_programs(0))
    def _():
        _start_row_gather(y_hbm, pos_next_ref, ybuf, 1 - slot, sem, TOP_K * tm)

    _wait_row_gather(y_hbm, pos_ref, ybuf, slot, sem, TOP_K * tm)
    wt = wt_ref[...]
    moe = wt[:, 0:1] * ybuf[slot, pl.ds(0, tm), :] + wt[:, 1:2] * ybuf[slot, pl.ds(tm, tm), :]
    out = _layer_norm(ALPHA * xf_ref[...] + moe, g_ref[...], b_ref[...])
    of_ref[...] = out
    ob_ref[...] = out.astype(ob_ref.dtype)


def _combine(y_sorted, pos_tiles, xf, wts, ln_g, ln_b, layer):
    t = xf.shape[0]
    nt = pos_tiles.shape[0]
    tm = t // nt
    row_spec = pl.BlockSpec((tm, D_MODEL), lambda i: (i, 0))
    vec_spec = pl.BlockSpec((None, 1, D_MODEL), lambda i: (layer, 0, 0))
    return pl.pallas_call(
        _combine_kernel,
        grid=(nt,),
        in_specs=[pl.BlockSpec((None, 1, TOP_K * tm), lambda i: (i, 0, 0), memory_space=pltpu.SMEM),
                  pl.BlockSpec((None, 1, TOP_K * tm), lambda i: (jnp.minimum(i + 1, nt - 1), 0, 0),
                               memory_space=pltpu.SMEM),
                  pl.BlockSpec(memory_space=pl.ANY),
                  row_spec,
                  pl.BlockSpec((tm, LANES), lambda i: (i, 0)),
                  vec_spec, vec_spec],
        out_specs=[row_spec, row_spec],
        out_shape=[jax.ShapeDtypeStruct((t, D_MODEL), jnp.float32),
                   jax.ShapeDtypeStruct((t, D_MODEL), jnp.bfloat16)],
        scratch_shapes=[pltpu.VMEM((2, TOP_K * tm, D_MODEL), jnp.float32),
                        pltpu.SemaphoreType.DMA((2,))],
        compiler_params=_cparams(("arbitrary",), 48),
    )(pos_tiles, pos_tiles, y_sorted, xf, wts, ln_g, ln_b)


MOE_BLOCK_ROWS = 256
PROJ_TM, PROJ_TN = 2048, 512
MIX_TM, MIX_TN = 1024, 256
OUT_TM = 512
SG_TM = 512
ROUTER_TM = 512
ROW_DMA_TM = 256


def _mixer_layer(xf, xb, p, layer, batch, seq):
    t = batch * seq
    z_ml = _matmul(xb, p["w_ml"], layer, jnp.float32, PROJ_TM, PROJ_TN)
    z_g = _matmul(xb, p["w_mg"], layer, jnp.float32, PROJ_TM, LANES)
    z_at = _matmul(xb, p["w_at"], layer, jnp.bfloat16, PROJ_TM, PROJ_TN)
    z_sp = _matmul(xb, p["w_sp"], layer, jnp.float32, PROJ_TM, PROJ_TN)
    y_ml = _mlstm(z_ml.reshape(batch, seq, ML_W), z_g.reshape(batch, seq, LANES),
                  p["ml_conv_w"], p["ml_gate_b"], p["ml_norm_w"], layer, batch, seq).reshape(t, BRANCH_W)
    y_at = _dilated_attention(z_at.reshape(batch, seq, 3 * ATT_W), batch, seq).reshape(t, BRANCH_W)
    y_sg = _spatial_gating(z_sp, p["sg_ln_g"], p["sg_ln_b"], p["sg_w"], p["sg_b"], layer, SG_TM)
    y_pl = _multiscale_pool(z_sp.reshape(batch, seq, 3 * BRANCH_W), p["pool_w"], p["pool_scale"],
                            layer, batch, seq).reshape(t, BRANCH_W)
    merged = _gate_mix(xb, (y_ml, y_at, y_sg, y_pl), p["w_gate"], p["b_gate"], p["w_branch"], layer,
                       MIX_TM, MIX_TN)
    return _out_proj_ln(merged, p["w_out"], xf, p["ln1_g"], p["ln1_b"], layer, OUT_TM)


def _moe_layer(xf, xb, xs_prev, p, layer):
    info, wts, counts = _router(xf, p["w_router"], p["b_router"], layer, ROUTER_TM)
    dest, blk_e, n_used = _dispatch_plan(info, counts, MOE_BLOCK_ROWS, ROUTER_TM)
    dest_tiles = _dest_tiles(dest, ROW_DMA_TM)
    xs = _dispatch_rows(xb, dest_tiles, xs_prev)
    y_sorted = _experts(xs, blk_e, n_used, p["w_exp_gate"], p["w_exp_up"], p["w_exp_down"], layer, MOE_BLOCK_ROWS)
    xf, xb = _combine(y_sorted, dest_tiles, xf, wts, p["ln2_g"], p["ln2_b"], layer)
    return xf, xb, xs


def _prepare_params(w_in, ml_conv_w, ml_gate_b, ml_norm_w, sg_ln_g, sg_ln_b, sg_w, sg_b, pool_w, pool_scale,
                    w_gate, b_gate, w_branch, w_out, ln1_g, ln1_b, w_router_group, b_router_group,
                    w_router_expert, b_router_expert, w_exp_gate, w_exp_up, w_exp_down, ln2_g, ln2_b):
    bf = jnp.bfloat16
    depth = w_in.shape[0]
    c_ml, c_g = ML_W, ML_W + ML_GATES
    c_at = c_g + 3 * ATT_W
    row = lambda a: a.reshape(depth, 1, -1)
    lane_pad = lambda a: jnp.pad(a, [(0, 0)] * (a.ndim - 1) + [(0, LANES - a.shape[-1])])
    w_r = jnp.zeros((depth, D_MODEL, LANES), jnp.float32)
    w_r = w_r.at[:, :, :N_EGROUPS].set(w_router_group).at[:, :, ROUTER_EOFF:ROUTER_EOFF + N_EXPERTS].set(w_router_expert)
    b_r = jnp.zeros((depth, LANES), jnp.float32)
    b_r = b_r.at[:, :N_EGROUPS].set(b_router_group).at[:, ROUTER_EOFF:ROUTER_EOFF + N_EXPERTS].set(b_router_expert)
    w_r_hi = w_r.astype(bf)
    w_r_lo = (w_r - w_r_hi.astype(jnp.float32)).astype(bf)
    sg_b_exp = jnp.repeat(jnp.swapaxes(sg_b, 1, 2), BRANCH_W // SG_GROUPS, axis=2)
    return dict(
        w_ml=w_in[:, :, :c_ml].astype(bf),
        w_mg=lane_pad(w_in[:, :, c_ml:c_g]).astype(bf),
        w_at=w_in[:, :, c_g:c_at].astype(bf),
        w_sp=w_in[:, :, c_at:].astype(bf),
        ml_conv_w=ml_conv_w,
        ml_gate_b=lane_pad(ml_gate_b.reshape(depth, 1, ML_GATES)),
        ml_norm_w=row(ml_norm_w),
        sg_ln_g=row(sg_ln_g), sg_ln_b=row(sg_ln_b), sg_w=sg_w.astype(bf), sg_b=sg_b_exp,
        pool_w=pool_w.astype(bf), pool_scale=row(pool_scale),
        w_gate=w_gate.astype(bf), b_gate=row(b_gate), w_branch=w_branch.astype(bf), w_out=w_out.astype(bf),
        ln1_g=row(ln1_g), ln1_b=row(ln1_b),
        w_router=jnp.stack([w_r_hi, w_r_lo], axis=1), b_router=row(b_r),
        w_exp_gate=w_exp_gate, w_exp_up=w_exp_up, w_exp_down=w_exp_down,
        ln2_g=row(ln2_g), ln2_b=row(ln2_b))


def kernel(x, w_in, ml_conv_w, ml_gate_b, ml_norm_w, sg_ln_g, sg_ln_b, sg_w, sg_b, pool_w, pool_scale, w_gate,
           b_gate, w_branch, w_out, ln1_g, ln1_b, w_router_group, b_router_group, w_router_expert,
           b_router_expert, w_exp_gate, w_exp_up, w_exp_down, ln2_g, ln2_b):
    batch, seq, d = x.shape
    assert d == D_MODEL and seq % (ATT_DIL[-1] * ATT_QT) == 0
    p = _prepare_params(w_in, ml_conv_w, ml_gate_b, ml_norm_w, sg_ln_g, sg_ln_b, sg_w, sg_b, pool_w, pool_scale,
                        w_gate, b_gate, w_branch, w_out, ln1_g, ln1_b, w_router_group, b_router_group,
                        w_router_expert, b_router_expert, w_exp_gate, w_exp_up, w_exp_down, ln2_g, ln2_b)
    t = batch * seq
    xf = x.reshape(t, d)
    xb = xf.astype(jnp.bfloat16)
    n_slots = (t * TOP_K // MOE_BLOCK_ROWS + N_EXPERTS) * MOE_BLOCK_ROWS
    xs = jnp.zeros((n_slots, d // 2), jnp.uint32)
    for layer in range(w_in.shape[0]):
        xf, xb = _mixer_layer(xf, xb, p, layer, batch, seq)
        xf, xb, xs = _moe_layer(xf, xb, xs, p, layer)
    return xf.reshape(batch, seq, d)
```

```python
import functools

import numpy as np
import jax
import jax.numpy as jnp
from jax import lax
from jax.experimental import pallas as pl
from jax.experimental.pallas import tpu as pltpu

D_MODEL = 2048
DEPTH = 4
BRANCH_W = 512
N_BRANCH = 4

ML_HEADS = 4
ML_DQK = 64
ML_DV = 128
ML_CHUNK = 128
ML_QK_W = ML_HEADS * ML_DQK
ML_V_W = ML_HEADS * ML_DV
ML_GATES = 2 * 2 * ML_HEADS
ML_W = 2 * ML_QK_W + 2 * ML_V_W

ATT_WINDOW = (128, 512, 2048)
ATT_DIL = (1, 4, 16)
ATT_NGROUP = 3
ATT_HEADS = 8
ATT_DH = 64
ATT_GW = ATT_HEADS * ATT_DH
ATT_W = ATT_NGROUP * ATT_GW
ATT_NEIGH = 64
ATT_QT = 128

SG_CHUNK = 128
SG_GROUPS = 4
POOL_WINDOWS = (2, 4, 8, 16)
POOL_DG = BRANCH_W // len(POOL_WINDOWS)

N_EGROUPS = 4
EXP_PER_GROUP = 8
N_EXPERTS = N_EGROUPS * EXP_PER_GROUP
TOP_K = 2
D_EXPERT = 512
ROUTER_EOFF = 32

ALPHA = (2.0 * DEPTH) ** 0.25
LN_EPS = 1e-5
NEG = -1e30

ROW_DMA_UNROLL = 8

LANES = 128
VMEM_BYTES_V7X = 64 * 1024 * 1024


def _cparams(semantics, vmem_mb):
    assert vmem_mb * 1024 * 1024 < VMEM_BYTES_V7X
    return pltpu.CompilerParams(dimension_semantics=semantics, vmem_limit_bytes=vmem_mb * 1024 * 1024)


def _sigmoid(x):
    return 1.0 / (1.0 + jnp.exp(-x))


def _bdot(a, b):
    return jnp.dot(a, b, preferred_element_type=jnp.float32)


def _layer_norm(x, g, b):
    mu = jnp.mean(x, axis=-1, keepdims=True)
    xc = x - mu
    var = jnp.mean(xc * xc, axis=-1, keepdims=True)
    return xc * lax.rsqrt(var + LN_EPS) * g + b


def _mm_kernel(x_ref, w_ref, o_ref):
    o_ref[...] = _bdot(x_ref[...], w_ref[...]).astype(o_ref.dtype)


def _matmul(x, w, layer, out_dtype, tm, tn):
    t, k = x.shape
    n = w.shape[2]
    return pl.pallas_call(
        _mm_kernel,
        grid=(t // tm, n // tn),
        in_specs=[pl.BlockSpec((tm, k), lambda i, j: (i, 0)),
                  pl.BlockSpec((None, k, tn), lambda i, j: (layer, 0, j))],
        out_specs=pl.BlockSpec((tm, tn), lambda i, j: (i, j)),
        out_shape=jax.ShapeDtypeStruct((t, n), out_dtype),
        compiler_params=_cparams(("parallel", "arbitrary"), 40),
    )(x, w)


def _split_hi_lo(x):
    hi = x.astype(jnp.bfloat16)
    lo = (x - hi.astype(jnp.float32)).astype(jnp.bfloat16)
    return hi, lo


def _mlstm_kernel(q_ref, k_ref, v_ref, o_ref, g_ref, cw_ref, gb_ref, nw_ref, y_ref,
                  qc_s, kc_s, gp_s, hf_s, hb_s, ct_s):
    s_len = q_ref.shape[0]
    n_chunks = s_len // ML_CHUNK
    lc = ML_CHUNK

    row_c = lax.broadcasted_iota(jnp.int32, (lc, 1), 0)
    cw = cw_ref[...]
    gate_lane = lax.broadcasted_iota(jnp.int32, (lc, LANES), 1)

    def prep(c, carry):
        r0 = pl.multiple_of(c * lc, lc)
        rows = pl.ds(r0, lc)

        def conv_silu(ref, w):
            x = ref[rows, :]
            prev_row = jnp.where(c > 0, ref[pl.ds(jnp.maximum(r0 - 1, 0), 1), :], 0.0)
            next_row = jnp.where(c < n_chunks - 1, ref[pl.ds(jnp.minimum(r0 + lc, s_len - 1), 1), :], 0.0)
            x_prev = jnp.where(row_c == 0, prev_row, pltpu.roll(x, 1, 0))
            x_next = jnp.where(row_c == lc - 1, next_row, pltpu.roll(x, lc - 1, 0))
            y = x_prev * w[0:1] + x * w[1:2] + x_next * w[2:3]
            return y * _sigmoid(y)

        qc_s[rows, :] = (conv_silu(q_ref, cw[:, :ML_QK_W]) * (ML_DQK ** -0.5)).astype(jnp.bfloat16)
        kc_s[rows, :] = conv_silu(k_ref, cw[:, ML_QK_W:]).astype(jnp.bfloat16)

        gp = g_ref[rows, :] + gb_ref[...]
        log_sig = jnp.minimum(gp, 0.0) - jnp.log(1.0 + jnp.exp(-jnp.abs(gp)))
        gp_s[rows, :] = jnp.where((gate_lane % 8) >= ML_HEADS, log_sig, gp)
        return carry

    lax.fori_loop(0, n_chunks, prep, 0)

    r_i = lax.broadcasted_iota(jnp.int32, (lc, lc), 0)
    c_i = lax.broadcasted_iota(jnp.int32, (lc, lc), 1)
    lower = c_i <= r_i
    upper = c_i >= r_i
    tril = jnp.where(lower, 1.0, 0.0).astype(jnp.bfloat16)
    triu = jnp.where(upper, 1.0, 0.0).astype(jnp.bfloat16)
    ones_ext = jnp.ones((lc, ML_DV), jnp.bfloat16)
    head_of_lane = lax.broadcasted_iota(jnp.int32, (lc, ML_QK_W), 1) // ML_DQK
    even_lanes = lax.broadcasted_iota(jnp.int32, (1, LANES), 1) < ML_DQK
    even_rows = lax.broadcasted_iota(jnp.int32, (LANES, 1), 0) < ML_DQK

    ct_s[...] = jnp.zeros(ct_s.shape, ct_s.dtype)

    def body(c, carry):
        units = []
        for direction, h_out in ((0, hf_s), (1, hb_s)):
            cc = c if direction == 0 else n_chunks - 1 - c
            rows = pl.ds(pl.multiple_of(cc * lc, lc), lc)
            gc = gp_s[rows, :]
            gct = gc.T
            gc_hi, gc_lo = _split_hi_lo(gc)
            gct_hi, gct_lo = _split_hi_lo(gct)
            if direction == 0:
                cum_col = _bdot(tril, gc_hi) + _bdot(tril, gc_lo)
                cum_row = _bdot(gct_hi, triu) + _bdot(gct_lo, triu)
                mask = lower
            else:
                cum_col = _bdot(triu, gc_hi) + _bdot(triu, gc_lo)
                cum_row = _bdot(gct_hi, tril) + _bdot(gct_lo, tril)
                mask = upper
            qch = qc_s[rows, :]
            kch = kc_s[rows, :]
            vch = v_ref[rows, :].astype(jnp.bfloat16)
            for h in range(ML_HEADS):
                slab = slice((h // 2) * LANES, (h // 2 + 1) * LANES)
                own = even_lanes if h % 2 == 0 else jnp.logical_not(even_lanes)
                q_own = jnp.where(own, qch[:, slab], jnp.zeros((lc, LANES), jnp.bfloat16))
                idx = direction * ML_HEADS + h
                units.append(dict(
                    direction=direction, h=h, idx=idx, rows=rows, h_out=h_out, mask=mask, gc=gc, gct=gct,
                    cum_col=cum_col, cum_row=cum_row, kch=kch,
                    v_ext=jnp.concatenate([vch[:, h * ML_DV:(h + 1) * ML_DV], ones_ext], axis=1),
                    qk=lax.dot_general(q_own, kch[:, slab], (((1,), (1,)), ((), ())),
                                       preferred_element_type=jnp.float32),
                    qct=_bdot(q_own, ct_s[idx].astype(jnp.bfloat16))))

        for u in units:
            ci = u["direction"] * 8 + u["h"]
            cf = ci + ML_HEADS
            m = carry[u["idx"]]
            b_col = u["cum_col"][:, cf:cf + 1]
            b_row = u["cum_row"][cf:cf + 1, :]
            i_row = u["gct"][ci:ci + 1, :]
            i_col = u["gc"][:, ci:ci + 1]
            total = b_row[:, lc - 1:lc] if u["direction"] == 0 else b_row[:, 0:1]
            dmat = jnp.where(u["mask"], b_col - b_row + i_row, NEG)
            m_inter = m + b_col
            m_t = jnp.maximum(m_inter, jnp.max(dmat, axis=1, keepdims=True))
            u["w_inter"] = jnp.exp(m_inter - m_t)
            u["floor"] = jnp.exp(-m_t)
            u["s"] = (u["qk"] * jnp.exp(dmat - m_t)).astype(jnp.bfloat16)
            g_row = total - b_row + i_row
            m_new = jnp.maximum(m + total, jnp.max(g_row, axis=1, keepdims=True))
            u["decay"] = jnp.exp(m + total - m_new)
            u["wk_col"] = jnp.exp(total - b_col + i_col - m_new)
            u["m_new"] = m_new

        for u in units:
            both = u["w_inter"] * u["qct"] + _bdot(u["s"], u["v_ext"])
            num, den = both[:, :ML_DV], both[:, ML_DV:]
            u["h_out"][u["rows"], u["h"] * ML_DV:(u["h"] + 1) * ML_DV] = num / jnp.maximum(jnp.abs(den), u["floor"])

        for direction in (0, 1):
            us = units[direction * ML_HEADS:(direction + 1) * ML_HEADS]
            wk_all = us[0]["wk_col"]
            for h in range(1, ML_HEADS):
                wk_all = jnp.where(head_of_lane >= h, us[h]["wk_col"], wk_all)
            kwt = (us[0]["kch"].astype(jnp.float32) * wk_all).T.astype(jnp.bfloat16)
            for u in us:
                h = u["h"]
                own = even_rows if h % 2 == 0 else jnp.logical_not(even_rows)
                kw_own = jnp.where(own, kwt[(h // 2) * LANES:(h // 2 + 1) * LANES, :],
                                   jnp.zeros((LANES, lc), jnp.bfloat16))
                ct_s[u["idx"]] = u["decay"] * ct_s[u["idx"]] + _bdot(kw_own, u["v_ext"])
        return tuple(u["m_new"] for u in units)

    zero = jnp.zeros((1, 1), jnp.float32)
    lax.fori_loop(0, n_chunks, body, (zero,) * (2 * ML_HEADS))

    nw = nw_ref[...]

    def finish(c, carry):
        rows = pl.ds(pl.multiple_of(c * lc, lc), lc)
        hsum = hf_s[rows, :] + hb_s[rows, :]
        og = _sigmoid(o_ref[rows, :])
        for h in range(ML_HEADS):
            sl = slice(h * ML_DV, (h + 1) * ML_DV)
            hh = hsum[:, sl]
            mu = jnp.mean(hh, axis=-1, keepdims=True)
            hc = hh - mu
            var = jnp.mean(hc * hc, axis=-1, keepdims=True)
            y_ref[rows, sl] = (hc * lax.rsqrt(var + LN_EPS) * nw[:, sl] * og[:, sl]).astype(y_ref.dtype)
        return carry

    lax.fori_loop(0, n_chunks, finish, 0)


def _mlstm(z_ml, z_g, conv_w, gate_b, norm_w, layer, batch, seq):
    return pl.pallas_call(
        _mlstm_kernel,
        grid=(batch,),
        in_specs=[pl.BlockSpec((None, seq, ML_QK_W), lambda b: (b, 0, 0)),
                  pl.BlockSpec((None, seq, ML_QK_W), lambda b: (b, 0, 1)),
                  pl.BlockSpec((None, seq, ML_V_W), lambda b: (b, 0, 1)),
                  pl.BlockSpec((None, seq, ML_V_W), lambda b: (b, 0, 2)),
                  pl.BlockSpec((None, seq, LANES), lambda b: (b, 0, 0)),
                  pl.BlockSpec((None, 3, 2 * ML_QK_W), lambda b: (layer, 0, 0)),
                  pl.BlockSpec((None, 1, LANES), lambda b: (layer, 0, 0)),
                  pl.BlockSpec((None, 1, ML_V_W), lambda b: (layer, 0, 0))],
        out_specs=pl.BlockSpec((None, seq, ML_V_W), lambda b: (b, 0, 0)),
        out_shape=jax.ShapeDtypeStruct((batch, seq, ML_V_W), jnp.bfloat16),
        scratch_shapes=[pltpu.VMEM((seq, ML_QK_W), jnp.bfloat16),
                        pltpu.VMEM((seq, ML_QK_W), jnp.bfloat16),
                        pltpu.VMEM((seq, LANES), jnp.float32),
                        pltpu.VMEM((seq, ML_V_W), jnp.float32),
                        pltpu.VMEM((seq, ML_V_W), jnp.float32),
                        pltpu.VMEM((2 * ML_HEADS, 2 * ML_DQK, 2 * ML_DV), jnp.float32)],
        compiler_params=_cparams(("parallel",), 56),
    )(z_ml, z_ml, z_ml, z_ml, z_g, conv_w, gate_b, norm_w)


def _alibi_slopes():
    n = ATT_NGROUP * ATT_HEADS
    s = 2.0 ** (-8.0 * np.arange(1, n + 1) / n)
    return s.reshape(ATT_NGROUP, ATT_HEADS).astype(np.float32)


ATT_SLABS = ATT_GW // LANES
ATT_STAGE_ROWS = 256


def _attn_group(q_ref, k_ref, v_ref, y_ref, stage_s, qs, ks, vs, oacc, lacc, osub, lsub, mix_s, bias_s, *,
                group, seq, first, last, prev_dil):
    dil = ATT_DIL[group]
    assert ATT_WINDOW[group] // (2 * dil) == ATT_NEIGH and (not last or dil == 1)
    assert prev_dil is None or (prev_dil % dil == 0 and ATT_QT % (prev_dil // dil) == 0)
    sub_len = seq // dil
    slopes = [float(s) for s in _alibi_slopes()[group] * np.float32(dil)]
    kw = min(2 * ATT_QT, sub_len)
    tiles_per_sub = sub_len // ATT_QT

    if dil == 1:
        q_src, k_src, v_src = q_ref, k_ref, v_ref
    else:
        for src, dst in ((q_ref, qs), (k_ref, ks), (v_ref, vs)):
            def stage(c, carry, src=src):
                rows = pl.ds(pl.multiple_of(c * ATT_STAGE_ROWS, ATT_STAGE_ROWS), ATT_STAGE_ROWS)
                x = src[rows, :].astype(jnp.float32)
                for sl in range(ATT_SLABS):
                    stage_s[sl, rows, :] = x[:, sl * LANES:(sl + 1) * LANES]
                return carry

            lax.fori_loop(0, seq // ATT_STAGE_ROWS, stage, 0)

            def gather_residue(r, carry, dst=dst):
                out_rows = pl.ds(pl.multiple_of(r * sub_len, sub_len), sub_len)
                for sl in range(ATT_SLABS):
                    x = stage_s[sl, pl.ds(r, sub_len, stride=dil), :]
                    dst[out_rows, sl * LANES:(sl + 1) * LANES] = x.astype(jnp.bfloat16)
                return carry

            lax.fori_loop(0, dil, gather_residue, 0)
        q_src, k_src, v_src = qs, ks, vs

    low_half = lax.broadcasted_iota(jnp.int32, (1, LANES), 1) < ATT_DH

    n_offsets = 1 if tiles_per_sub == 1 else 3
    rel0 = (lax.broadcasted_iota(jnp.int32, (ATT_QT, kw), 1) - lax.broadcasted_iota(jnp.int32, (ATT_QT, kw), 0))
    for oi in range(n_offsets):
        arel = jnp.abs(rel0 - oi * ATT_NEIGH).astype(jnp.float32)
        for h in range(ATT_HEADS):
            bias_s[oi * ATT_HEADS + h, :, 0:kw] = jnp.where(arel <= float(ATT_NEIGH), -(slopes[h] * arel), NEG)

    def tile(idx, carry):
        r = idx // tiles_per_sub
        q0 = (idx % tiles_per_sub) * ATT_QT
        k0 = jnp.clip(q0 - ATT_NEIGH, 0, sub_len - kw)
        base = r * sub_len
        bias_base = (q0 - k0) // ATT_NEIGH * ATT_HEADS
        q = q_src[pl.ds(pl.multiple_of(base + q0, ATT_QT), ATT_QT), :] * (ATT_DH ** -0.5)
        k = k_src[pl.ds(pl.multiple_of(base + k0, ATT_NEIGH), kw), :]
        v = v_src[pl.ds(pl.multiple_of(base + k0, ATT_NEIGH), kw), :]
        start = q0 * dil + r
        nat_rows = pl.ds(start, ATT_QT) if dil == 1 else pl.ds(start, ATT_QT, stride=dil)
        scores = []
        for sl in range(ATT_SLABS):
            cols = slice(sl * LANES, (sl + 1) * LANES)
            q2, k2 = q[:, cols], k[:, cols]
            for half in range(2):
                own = low_half if half == 0 else jnp.logical_not(low_half)
                scores.append(lax.dot_general(jnp.where(own, q2, jnp.zeros_like(q2)), k2,
                                              (((1,), (1,)), ((), ())), preferred_element_type=jnp.float32))
        probs = []
        for h in range(ATT_HEADS):
            s = scores[h] + bias_s[bias_base + h, :, 0:kw]
            m = jnp.max(s, axis=1, keepdims=True)
            p = jnp.exp(s - m)
            den = jnp.sum(p, axis=1, keepdims=True)
            probs.append((p.astype(jnp.bfloat16), den, m + jnp.log(den)))
        for sl in range(ATT_SLABS):
            cols = slice(sl * LANES, (sl + 1) * LANES)
            v2 = v[:, cols]
            (p0, den0, lse0), (p1, den1, lse1) = probs[2 * sl], probs[2 * sl + 1]
            o_sl = jnp.where(low_half, _bdot(p0, v2) / den0, _bdot(p1, v2) / den1)
            l_sl = jnp.where(low_half, lse0, lse1)
            if first:
                sub_rows = pl.ds(pl.multiple_of(base + q0, ATT_QT), ATT_QT)
                osub[sl, sub_rows, :] = o_sl
                lsub[sl, sub_rows, :] = l_sl
                continue
            if prev_dil is not None:
                ratio = prev_dil // dil
                seg = ATT_QT // ratio
                for c in range(ratio):
                    src_rows = pl.ds(pl.multiple_of((dil * c + r) * (seq // prev_dil) + q0 // ratio, seg), seg)
                    mix_s[0, pl.ds(c, seg, stride=ratio), :] = osub[sl, src_rows, :]
                    mix_s[1, pl.ds(c, seg, stride=ratio), :] = lsub[sl, src_rows, :]
                o_old, l_old = mix_s[0], mix_s[1]
            else:
                o_old, l_old = oacc[sl, nat_rows, :], lacc[sl, nat_rows, :]
            mx = jnp.maximum(l_old, l_sl)
            w_old, w_new = jnp.exp(l_old - mx), jnp.exp(l_sl - mx)
            tot = w_old + w_new
            o_sl = (w_old * o_old + w_new * o_sl) / tot
            l_sl = mx + jnp.log(tot)
            if last:
                y_ref[pl.ds(pl.multiple_of(start, ATT_QT), ATT_QT), cols] = o_sl.astype(y_ref.dtype)
            else:
                oacc[sl, nat_rows, :] = o_sl
                lacc[sl, nat_rows, :] = l_sl
        return carry

    lax.fori_loop(0, seq // ATT_QT, tile, 0)


def _attn_kernel(q_ref, k_ref, v_ref, y_ref, *scratch, seq):
    gi = pl.program_id(1)
    for step in range(ATT_NGROUP):
        @pl.when(gi == step)
        def _(step=step):
            group = ATT_NGROUP - 1 - step
            _attn_group(q_ref, k_ref, v_ref, y_ref, *scratch, group=group, seq=seq,
                        first=step == 0, last=step == ATT_NGROUP - 1,
                        prev_dil=ATT_DIL[group + 1] if step == 1 else None)


def _dilated_attention(z_at, batch, seq):
    def in_spec(which):
        return pl.BlockSpec((None, seq, ATT_GW),
                            lambda b, gi: (b, 0, which * ATT_NGROUP + ATT_NGROUP - 1 - gi))

    slab = pltpu.VMEM((ATT_SLABS, seq, LANES), jnp.float32)
    sub = pltpu.VMEM((seq, ATT_GW), jnp.bfloat16)
    return pl.pallas_call(
        functools.partial(_attn_kernel, seq=seq),
        grid=(batch, ATT_NGROUP),
        in_specs=[in_spec(0), in_spec(1), in_spec(2)],
        out_specs=pl.BlockSpec((None, seq, ATT_GW), lambda b, gi: (b, 0, 0)),
        out_shape=jax.ShapeDtypeStruct((batch, seq, ATT_GW), jnp.bfloat16),
        scratch_shapes=[slab, sub, sub, sub, slab, slab, slab, slab,
                        pltpu.VMEM((2, ATT_QT, LANES), jnp.float32),
                        pltpu.VMEM((3 * ATT_HEADS, ATT_QT, 2 * ATT_QT), jnp.float32)],
        compiler_params=_cparams(("parallel", "arbitrary"), 56),
    )(z_at, z_at, z_at)


def _gelu_tanh(x):
    return x * (0.5 * (1.0 + jnp.tanh(np.sqrt(2.0 / np.pi).astype(np.float32) * (x + 0.044715 * (x * x * x)))))


def _sg_kernel(u_ref, v_ref, g_ref, b_ref, ws_ref, bs_ref, y_ref):
    tm = u_ref.shape[0]
    dg = BRANCH_W // SG_GROUPS
    u = _gelu_tanh(u_ref[...])
    vn = _layer_norm(_gelu_tanh(v_ref[...]), g_ref[...], b_ref[...]).astype(jnp.bfloat16)
    bs = bs_ref[...]
    for c in range(tm // SG_CHUNK):
        rows = slice(c * SG_CHUNK, (c + 1) * SG_CHUNK)
        for g in range(SG_GROUPS):
            cols = slice(g * dg, (g + 1) * dg)
            mixed = _bdot(ws_ref[g], vn[rows, cols]) + bs[:, cols]
            y_ref[rows, cols] = (u[rows, cols] * mixed).astype(y_ref.dtype)


def _spatial_gating(z_sp, ln_g, ln_b, w_s, b_s_exp, layer, tm):
    t = z_sp.shape[0]
    return pl.pallas_call(
        _sg_kernel,
        grid=(t // tm,),
        in_specs=[pl.BlockSpec((tm, BRANCH_W), lambda i: (i, 0)),
                  pl.BlockSpec((tm, BRANCH_W), lambda i: (i, 1)),
                  pl.BlockSpec((None, 1, BRANCH_W), lambda i: (layer, 0, 0)),
                  pl.BlockSpec((None, 1, BRANCH_W), lambda i: (layer, 0, 0)),
                  pl.BlockSpec((None, SG_GROUPS, SG_CHUNK, SG_CHUNK), lambda i: (layer, 0, 0, 0)),
                  pl.BlockSpec((None, SG_CHUNK, BRANCH_W), lambda i: (layer, 0, 0))],
        out_specs=pl.BlockSpec((tm, BRANCH_W), lambda i: (i, 0)),
        out_shape=jax.ShapeDtypeStruct((t, BRANCH_W), jnp.bfloat16),
        compiler_params=_cparams(("parallel",), 32),
    )(z_sp, z_sp, ln_g, ln_b, w_s, b_s_exp)


def _pool_kernel(p_ref, w_ref, sc_ref, y_ref):
    s_len = p_ref.shape[0]
    row = lax.broadcasted_iota(jnp.int32, (s_len, 1), 0)
    sc = sc_ref[...]
    for g, win in enumerate(POOL_WINDOWS):
        cols = slice(g * POOL_DG, (g + 1) * POOL_DG)
        half = win // 2
        p = p_ref[:, cols]
        acc = jnp.zeros_like(p)
        for j in range(-half, half):
            if j == 0:
                acc = acc + p
            else:
                shifted = pltpu.roll(p, (-j) % s_len, 0)
                ok = (row + j >= 0) & (row + j < s_len)
                acc = acc + jnp.where(ok, shifted, 0.0)
        cnt = (jnp.minimum(row + half, s_len) - jnp.maximum(row - half, 0)).astype(jnp.float32)
        d = acc / cnt - p
        y = _bdot(d.astype(jnp.bfloat16), w_ref[g]) * sc[:, cols]
        y_ref[:, cols] = y.astype(y_ref.dtype)


def _multiscale_pool(z_sp, pool_w, pool_scale, layer, batch, seq):
    return pl.pallas_call(
        _pool_kernel,
        grid=(batch,),
        in_specs=[pl.BlockSpec((None, seq, BRANCH_W), lambda b: (b, 0, 2)),
                  pl.BlockSpec((None, len(POOL_WINDOWS), POOL_DG, POOL_DG), lambda b: (layer, 0, 0, 0)),
                  pl.BlockSpec((None, 1, BRANCH_W), lambda b: (layer, 0, 0))],
        out_specs=pl.BlockSpec((None, seq, BRANCH_W), lambda b: (b, 0, 0)),
        out_shape=jax.ShapeDtypeStruct((batch, seq, BRANCH_W), jnp.bfloat16),
        compiler_params=_cparams(("parallel",), 48),
    )(z_sp, pool_w, pool_scale)


def _gate_mix_kernel(xb_ref, y0, y1, y2, y3, wg0, wg1, wg2, wg3, bg0, bg1, bg2, bg3, wb_ref, o_ref):
    xb = xb_ref[...]
    merged = None
    for n, (y, wg, bg) in enumerate(((y0, wg0, bg0), (y1, wg1, bg1), (y2, wg2, bg2), (y3, wg3, bg3))):
        gate = _sigmoid(_bdot(xb, wg[...]) + bg[...])
        term = gate * _bdot(y[...], wb_ref[n])
        merged = term if merged is None else merged + term
    o_ref[...] = merged.astype(o_ref.dtype)


def _gate_mix(xb, ys, w_gate, b_gate, w_branch, layer, tm, tn):
    t = xb.shape[0]
    nj = D_MODEL // tn
    row_spec = lambda w: pl.BlockSpec((tm, w), lambda i, j: (i, 0))
    wg_specs = [pl.BlockSpec((None, D_MODEL, tn), lambda i, j, n=n: (layer, 0, n * nj + j)) for n in range(N_BRANCH)]
    bg_specs = [pl.BlockSpec((None, 1, tn), lambda i, j, n=n: (layer, 0, n * nj + j)) for n in range(N_BRANCH)]
    return pl.pallas_call(
        _gate_mix_kernel,
        grid=(t // tm, nj),
        in_specs=[row_spec(D_MODEL)] + [row_spec(BRANCH_W)] * N_BRANCH + wg_specs + bg_specs
                 + [pl.BlockSpec((None, N_BRANCH, BRANCH_W, tn), lambda i, j: (layer, 0, 0, j))],
        out_specs=pl.BlockSpec((tm, tn), lambda i, j: (i, j)),
        out_shape=jax.ShapeDtypeStruct((t, D_MODEL), jnp.bfloat16),
        compiler_params=_cparams(("parallel", "arbitrary"), 48),
    )(xb, *ys, *([w_gate] * N_BRANCH), *([b_gate] * N_BRANCH), w_branch)


def _out_proj_ln_kernel(m_ref, wo_ref, xf_ref, g_ref, b_ref, of_ref, ob_ref):
    out = _layer_norm(ALPHA * xf_ref[...] + _bdot(m_ref[...], wo_ref[...]), g_ref[...], b_ref[...])
    of_ref[...] = out
    ob_ref[...] = out.astype(ob_ref.dtype)


def _out_proj_ln(merged, w_out, xf, ln_g, ln_b, layer, tm):
    t = xf.shape[0]
    row_spec = pl.BlockSpec((tm, D_MODEL), lambda i: (i, 0))
    vec_spec = pl.BlockSpec((None, 1, D_MODEL), lambda i: (layer, 0, 0))
    return pl.pallas_call(
        _out_proj_ln_kernel,
        grid=(t // tm,),
        in_specs=[row_spec, pl.BlockSpec((None, D_MODEL, D_MODEL), lambda i: (layer, 0, 0)),
                  row_spec, vec_spec, vec_spec],
        out_specs=[row_spec, row_spec],
        out_shape=[jax.ShapeDtypeStruct((t, D_MODEL), jnp.float32),
                   jax.ShapeDtypeStruct((t, D_MODEL), jnp.bfloat16)],
        compiler_params=_cparams(("parallel",), 48),
    )(merged, w_out, xf, ln_g, ln_b)


def _router_kernel(x_ref, w_ref, b_ref, id_ref, wt_ref, cnt_ref):
    x_hi, x_lo = _split_hi_lo(x_ref[...])
    w_hi, w_lo = w_ref[0], w_ref[1]
    logits = _bdot(x_hi, w_hi) + _bdot(x_lo, w_hi) + _bdot(x_hi, w_lo) + b_ref[...]
    lane = lax.broadcasted_iota(jnp.int32, logits.shape, 1)
    lane_f = lane.astype(jnp.float32)
    no_lane = float(LANES)

    def first_argmax(vals):
        mx = jnp.max(vals, axis=1, keepdims=True)
        idx = jnp.min(jnp.where(vals == mx, lane_f, no_lane), axis=1, keepdims=True)
        return mx, idx.astype(jnp.int32)

    is_group = lane < N_EGROUPS
    g_max, g_sel = first_argmax(jnp.where(is_group, logits, NEG))
    p_sel = 1.0 / jnp.sum(jnp.where(is_group, jnp.exp(logits - g_max), 0.0), axis=1, keepdims=True)

    lo = ROUTER_EOFF + g_sel * EXP_PER_GROUP
    cand = jnp.where((lane >= lo) & (lane < lo + EXP_PER_GROUP), logits, NEG)
    l1, i1 = first_argmax(cand)
    l2, i2 = first_argmax(jnp.where(lane == i1, NEG, cand))
    e = jnp.exp(l2 - l1)
    w1 = p_sel / (1.0 + e)
    w2 = p_sel * e / (1.0 + e)
    e1, e2 = i1 - ROUTER_EOFF, i2 - ROUTER_EOFF

    tm = logits.shape[0]
    hit1, hit2 = lane == e1, lane == e2
    hits = jnp.where(jnp.logical_or(hit1, hit2), 1.0, 0.0)
    earlier = (lax.broadcasted_iota(jnp.int32, (tm, tm), 1) < lax.broadcasted_iota(jnp.int32, (tm, tm), 0))
    before = _bdot(jnp.where(earlier, 1.0, 0.0).astype(jnp.bfloat16), hits.astype(jnp.bfloat16))
    rank1 = jnp.sum(jnp.where(hit1, before, 0.0), axis=1, keepdims=True).astype(jnp.int32)
    rank2 = jnp.sum(jnp.where(hit2, before, 0.0), axis=1, keepdims=True).astype(jnp.int32)
    counts = jnp.sum(hits, axis=0, keepdims=True).astype(jnp.int32)

    id_ref[...] = jnp.where(lane == 0, e1, jnp.where(lane == 1, e2,
                            jnp.where(lane == 2, rank1, jnp.where(lane == 3, rank2, 0))))
    wt_ref[...] = jnp.where(lane == 0, w1, jnp.where(lane == 1, w2, 0.0))
    cnt_ref[...] = jnp.broadcast_to(counts, cnt_ref.shape)


def _router(xf, w_r, b_r, layer, tm):
    t = xf.shape[0]
    out_spec = pl.BlockSpec((tm, LANES), lambda i: (i, 0))
    return pl.pallas_call(
        _router_kernel,
        grid=(t // tm,),
        in_specs=[pl.BlockSpec((tm, D_MODEL), lambda i: (i, 0)),
                  pl.BlockSpec((None, 2, D_MODEL, LANES), lambda i: (layer, 0, 0, 0)),
                  pl.BlockSpec((None, 1, LANES), lambda i: (layer, 0, 0))],
        out_specs=[out_spec, out_spec, pl.BlockSpec((None, 8, LANES), lambda i: (i, 0, 0))],
        out_shape=[jax.ShapeDtypeStruct((t, LANES), jnp.int32),
                   jax.ShapeDtypeStruct((t, LANES), jnp.float32),
                   jax.ShapeDtypeStruct((t // tm, 8, LANES), jnp.int32)],
        compiler_params=_cparams(("parallel",), 32),
    )(xf, w_r, b_r)


def _dispatch_plan(info, counts, blk, tm):
    t = info.shape[0]
    c = counts[:, 0, :N_EXPERTS]
    tile_off = jnp.cumsum(c, axis=0) - c
    total = jnp.sum(c, axis=0)
    padded = (total + blk - 1) // blk * blk
    pad_end = jnp.cumsum(padded)
    base = (pad_end - padded)[None, :] + tile_off
    base_tok = jnp.repeat(base, tm, axis=0)
    hit = info[:, :TOP_K, None] == jnp.arange(N_EXPERTS, dtype=jnp.int32)[None, None, :]
    dest = jnp.sum(jnp.where(hit, base_tok[:, None, :], 0), axis=2) + info[:, TOP_K:2 * TOP_K]
    nb = t * TOP_K // blk + N_EXPERTS
    blk_start = jnp.arange(nb, dtype=jnp.int32) * blk
    blk_e = jnp.minimum(jnp.sum((pad_end[None, :] <= blk_start[:, None]).astype(jnp.int32), axis=1),
                        N_EXPERTS - 1)
    n_used = (pad_end[-1] // blk).astype(jnp.int32).reshape(1)
    return dest.astype(jnp.int32), blk_e, n_used


def _dispatch_kernel(dest_ref, dest_prev_ref, x_ref, xs_in, xs_out, pk, sem):
    del xs_in
    i = pl.program_id(0)
    tm = x_ref.shape[0]
    half = D_MODEL // 2
    slot = lax.rem(i, 2)
    x = x_ref[...].astype(jnp.float32)
    hi = lax.bitcast_convert_type(x[:, :half], jnp.uint32)
    lo = lax.bitcast_convert_type(x[:, half:], jnp.uint32)
    pk[slot] = hi | (lo >> 16)

    def copy(idx_ref, s, r, j):
        return pltpu.make_async_copy(pk.at[s, pl.ds(r, 1), :], xs_out.at[pl.ds(idx_ref[0, j], 1), :], sem.at[s])

    def start(g, carry):
        for u in range(ROW_DMA_UNROLL):
            j = g * ROW_DMA_UNROLL + u
            copy(dest_ref, slot, lax.rem(j, tm), j).start(priority=u % 2)
        return carry

    lax.fori_loop(0, TOP_K * tm // ROW_DMA_UNROLL, start, 0)

    def wait_prev(j, carry):
        copy(dest_prev_ref, 1 - slot, lax.rem(j, tm), j).wait()
        return carry

    @pl.when(i > 0)
    def _():
        lax.fori_loop(0, TOP_K * tm, wait_prev, 0, unroll=8)

    def wait_own(j, carry):
        copy(dest_ref, slot, lax.rem(j, tm), j).wait()
        return carry

    @pl.when(i == pl.num_programs(0) - 1)
    def _():
        lax.fori_loop(0, TOP_K * tm, wait_own, 0, unroll=8)


def _dest_tiles(dest, tm):
    nt = dest.shape[0] // tm
    return dest.reshape(nt, tm, TOP_K).transpose(0, 2, 1).reshape(nt, 1, TOP_K * tm)


def _dispatch_rows(xb, dest_tiles, xs_prev):
    t = xb.shape[0]
    nt, _, two_tm = dest_tiles.shape
    tm = two_tm // TOP_K
    return pl.pallas_call(
        _dispatch_kernel,
        grid=(nt,),
        in_specs=[pl.BlockSpec((None, 1, two_tm), lambda i: (i, 0, 0), memory_space=pltpu.SMEM),
                  pl.BlockSpec((None, 1, two_tm), lambda i: (jnp.maximum(i - 1, 0), 0, 0),
                               memory_space=pltpu.SMEM),
                  pl.BlockSpec((tm, D_MODEL), lambda i: (i, 0)),
                  pl.BlockSpec(memory_space=pl.ANY)],
        out_specs=pl.BlockSpec(memory_space=pl.ANY),
        out_shape=jax.ShapeDtypeStruct(xs_prev.shape, xs_prev.dtype),
        scratch_shapes=[pltpu.VMEM((2, tm, D_MODEL // 2), jnp.uint32),
                        pltpu.SemaphoreType.DMA((2,))],
        input_output_aliases={3: 0},
        compiler_params=_cparams(("arbitrary",), 32),
    )(dest_tiles, dest_tiles, xb, xs_prev)


def _row_copy(src_hbm, idx, buf, slot, r, sem):
    return pltpu.make_async_copy(src_hbm.at[pl.ds(idx, 1), :], buf.at[slot, pl.ds(r, 1), :], sem.at[slot])


def _start_row_gather(src_hbm, idx_ref, buf, slot, sem, n_rows):
    def body(g, carry):
        for u in range(ROW_DMA_UNROLL):
            r = g * ROW_DMA_UNROLL + u
            _row_copy(src_hbm, idx_ref[0, r], buf, slot, r, sem).start(priority=u % 2)
        return carry
    lax.fori_loop(0, n_rows // ROW_DMA_UNROLL, body, 0)


def _wait_row_gather(src_hbm, idx_ref, buf, slot, sem, n_rows):
    def body(r, carry):
        _row_copy(src_hbm, idx_ref[0, r], buf, slot, r, sem).wait()
        return carry
    lax.fori_loop(0, n_rows, body, 0, unroll=8)


def _expert_kernel(blk_e_ref, n_used_ref, xs_ref, w1_ref, w3_ref, w2_ref, y_ref, w1_s, w3_s, w2_s):
    i = pl.program_id(0)
    n_used = n_used_ref[0]
    new_expert = jnp.logical_or(i == 0, blk_e_ref[i] != blk_e_ref[jnp.maximum(i - 1, 0)])

    @pl.when(jnp.logical_and(i < n_used, new_expert))
    def _():
        w1_s[...] = w1_ref[...].astype(jnp.bfloat16)
        w3_s[...] = w3_ref[...].astype(jnp.bfloat16)
        w2_s[...] = w2_ref[...].astype(jnp.bfloat16)

    @pl.when(i < n_used)
    def _():
        packed = xs_ref[...]
        x_a = lax.bitcast_convert_type(packed & jnp.uint32(0xFFFF0000), jnp.float32)
        x_b = lax.bitcast_convert_type(packed << 16, jnp.float32)
        xb = jnp.concatenate([x_a, x_b], axis=1).astype(jnp.bfloat16)
        a = _bdot(xb, w1_s[...])
        h = (a * _sigmoid(a)) * _bdot(xb, w3_s[...])
        y_ref[...] = _bdot(h.astype(jnp.bfloat16), w2_s[...])

    @pl.when(i >= n_used)
    def _():
        y_ref[...] = jnp.zeros(y_ref.shape, y_ref.dtype)


def _experts(xs, blk_e, n_used, w1, w3, w2, layer, blk):
    nb = xs.shape[0] // blk

    def used(i, nu):
        return jnp.minimum(i, nu[0] - 1)

    def w_index(i, be, nu):
        return (layer, be[used(i, nu)], 0, 0)

    grid_spec = pltpu.PrefetchScalarGridSpec(
        num_scalar_prefetch=2,
        grid=(nb,),
        in_specs=[pl.BlockSpec((blk, D_MODEL // 2), lambda i, be, nu: (used(i, nu), 0)),
                  pl.BlockSpec((None, None, D_MODEL, D_EXPERT), w_index),
                  pl.BlockSpec((None, None, D_MODEL, D_EXPERT), w_index),
                  pl.BlockSpec((None, None, D_EXPERT, D_MODEL), w_index)],
        out_specs=pl.BlockSpec((blk, D_MODEL), lambda i, be, nu: (i, 0)),
        scratch_shapes=[pltpu.VMEM((D_MODEL, D_EXPERT), jnp.bfloat16),
                        pltpu.VMEM((D_MODEL, D_EXPERT), jnp.bfloat16),
                        pltpu.VMEM((D_EXPERT, D_MODEL), jnp.bfloat16)])
    return pl.pallas_call(
        _expert_kernel,
        grid_spec=grid_spec,
        out_shape=jax.ShapeDtypeStruct((nb * blk, D_MODEL), jnp.float32),
        compiler_params=_cparams(("arbitrary",), 56),
    )(blk_e, n_used, xs, w1, w3, w2)


def _combine_kernel(pos_ref, pos_next_ref, y_hbm, xf_ref, wt_ref, g_ref, b_ref, of_ref, ob_ref, ybuf, sem):
    i = pl.program_id(0)
    tm = xf_ref.shape[0]
    slot = lax.rem(i, 2)

    @pl.when(i == 0)
    def _():
        _start_row_gather(y_hbm, pos_ref, ybuf, 0, sem, TOP_K * tm)

    @pl.when(i + 1 < pl.num_programs(0))
    def _():
        _start_row_gather(y_hbm, pos_next_ref, ybuf, 1 - slot, sem, TOP_K * tm)

    _wait_row_gather(y_hbm, pos_ref, ybuf, slot, sem, TOP_K * tm)
    wt = wt_ref[...]
    moe = wt[:, 0:1] * ybuf[slot, pl.ds(0, tm), :] + wt[:, 1:2] * ybuf[slot, pl.ds(tm, tm), :]
    out = _layer_norm(ALPHA * xf_ref[...] + moe, g_ref[...], b_ref[...])
    of_ref[...] = out
    ob_ref[...] = out.astype(ob_ref.dtype)


def _combine(y_sorted, pos_tiles, xf, wts, ln_g, ln_b, layer):
    t = xf.shape[0]
    nt = pos_tiles.shape[0]
    tm = t // nt
    row_spec = pl.BlockSpec((tm, D_MODEL), lambda i: (i, 0))
    vec_spec = pl.BlockSpec((None, 1, D_MODEL), lambda i: (layer, 0, 0))
    return pl.pallas_call(
        _combine_kernel,
        grid=(nt,),
        in_specs=[pl.BlockSpec((None, 1, TOP_K * tm), lambda i: (i, 0, 0), memory_space=pltpu.SMEM),
                  pl.BlockSpec((None, 1, TOP_K * tm), lambda i: (jnp.minimum(i + 1, nt - 1), 0, 0),
                               memory_space=pltpu.SMEM),
                  pl.BlockSpec(memory_space=pl.ANY),
                  row_spec,
                  pl.BlockSpec((tm, LANES), lambda i: (i, 0)),
                  vec_spec, vec_spec],
        out_specs=[row_spec, row_spec],
        out_shape=[jax.ShapeDtypeStruct((t, D_MODEL), jnp.float32),
                   jax.ShapeDtypeStruct((t, D_MODEL), jnp.bfloat16)],
        scratch_shapes=[pltpu.VMEM((2, TOP_K * tm, D_MODEL), jnp.float32),
                        pltpu.SemaphoreType.DMA((2,))],
        compiler_params=_cparams(("arbitrary",), 48),
    )(pos_tiles, pos_tiles, y_sorted, xf, wts, ln_g, ln_b)


MOE_BLOCK_ROWS = 256
PROJ_TM, PROJ_TN = 2048, 512
MIX_TM, MIX_TN = 1024, 256
OUT_TM = 512
SG_TM = 512
ROUTER_TM = 512
ROW_DMA_TM = 256


def _mixer_layer(xf, xb, p, layer, batch, seq):
    t = batch * seq
    z_ml = _matmul(xb, p["w_ml"], layer, jnp.float32, PROJ_TM, PROJ_TN)
    z_g = _matmul(xb, p["w_mg"], layer, jnp.float32, PROJ_TM, LANES)
    z_at = _matmul(xb, p["w_at"], layer, jnp.bfloat16, PROJ_TM, PROJ_TN)
    z_sp = _matmul(xb, p["w_sp"], layer, jnp.float32, PROJ_TM, PROJ_TN)
    y_ml = _mlstm(z_ml.reshape(batch, seq, ML_W), z_g.reshape(batch, seq, LANES),
                  p["ml_conv_w"], p["ml_gate_b"], p["ml_norm_w"], layer, batch, seq).reshape(t, BRANCH_W)
    y_at = _dilated_attention(z_at.reshape(batch, seq, 3 * ATT_W), batch, seq).reshape(t, BRANCH_W)
    y_sg = _spatial_gating(z_sp, p["sg_ln_g"], p["sg_ln_b"], p["sg_w"], p["sg_b"], layer, SG_TM)
    y_pl = _multiscale_pool(z_sp.reshape(batch, seq, 3 * BRANCH_W), p["pool_w"], p["pool_scale"],
                            layer, batch, seq).reshape(t, BRANCH_W)
    merged = _gate_mix(xb, (y_ml, y_at, y_sg, y_pl), p["w_gate"], p["b_gate"], p["w_branch"], layer,
                       MIX_TM, MIX_TN)
    return _out_proj_ln(merged, p["w_out"], xf, p["ln1_g"], p["ln1_b"], layer, OUT_TM)


def _moe_layer(xf, xb, xs_prev, p, layer):
    info, wts, counts = _router(xf, p["w_router"], p["b_router"], layer, ROUTER_TM)
    dest, blk_e, n_used = _dispatch_plan(info, counts, MOE_BLOCK_ROWS, ROUTER_TM)
    dest_tiles = _dest_tiles(dest, ROW_DMA_TM)
    xs = _dispatch_rows(xb, dest_tiles, xs_prev)
    y_sorted = _experts(xs, blk_e, n_used, p["w_exp_gate"], p["w_exp_up"], p["w_exp_down"], layer, MOE_BLOCK_ROWS)
    xf, xb = _combine(y_sorted, dest_tiles, xf, wts, p["ln2_g"], p["ln2_b"], layer)
    return xf, xb, xs


def _prepare_params(w_in, ml_conv_w, ml_gate_b, ml_norm_w, sg_ln_g, sg_ln_b, sg_w, sg_b, pool_w, pool_scale,
                    w_gate, b_gate, w_branch, w_out, ln1_g, ln1_b, w_router_group, b_router_group,
                    w_router_expert, b_router_expert, w_exp_gate, w_exp_up, w_exp_down, ln2_g, ln2_b):
    bf = jnp.bfloat16
    depth = w_in.shape[0]
    c_ml, c_g = ML_W, ML_W + ML_GATES
    c_at = c_g + 3 * ATT_W
    row = lambda a: a.reshape(depth, 1, -1)
    lane_pad = lambda a: jnp.pad(a, [(0, 0)] * (a.ndim - 1) + [(0, LANES - a.shape[-1])])
    w_r = jnp.zeros((depth, D_MODEL, LANES), jnp.float32)
    w_r = w_r.at[:, :, :N_EGROUPS].set(w_router_group).at[:, :, ROUTER_EOFF:ROUTER_EOFF + N_EXPERTS].set(w_router_expert)
    b_r = jnp.zeros((depth, LANES), jnp.float32)
    b_r = b_r.at[:, :N_EGROUPS].set(b_router_group).at[:, ROUTER_EOFF:ROUTER_EOFF + N_EXPERTS].set(b_router_expert)
    w_r_hi = w_r.astype(bf)
    w_r_lo = (w_r - w_r_hi.astype(jnp.float32)).astype(bf)
    sg_b_exp = jnp.repeat(jnp.swapaxes(sg_b, 1, 2), BRANCH_W // SG_GROUPS, axis=2)
    return dict(
        w_ml=w_in[:, :, :c_ml].astype(bf),
        w_mg=lane_pad(w_in[:, :, c_ml:c_g]).astype(bf),
        w_at=w_in[:, :, c_g:c_at].astype(bf),
        w_sp=w_in[:, :, c_at:].astype(bf),
        ml_conv_w=ml_conv_w,
        ml_gate_b=lane_pad(ml_gate_b.reshape(depth, 1, ML_GATES)),
        ml_norm_w=row(ml_norm_w),
        sg_ln_g=row(sg_ln_g), sg_ln_b=row(sg_ln_b), sg_w=sg_w.astype(bf), sg_b=sg_b_exp,
        pool_w=pool_w.astype(bf), pool_scale=row(pool_scale),
        w_gate=w_gate.astype(bf), b_gate=row(b_gate), w_branch=w_branch.astype(bf), w_out=w_out.astype(bf),
        ln1_g=row(ln1_g), ln1_b=row(ln1_b),
        w_router=jnp.stack([w_r_hi, w_r_lo], axis=1), b_router=row(b_r),
        w_exp_gate=w_exp_gate, w_exp_up=w_exp_up, w_exp_down=w_exp_down,
        ln2_g=row(ln2_g), ln2_b=row(ln2_b))


def kernel(x, w_in, ml_conv_w, ml_gate_b, ml_norm_w, sg_ln_g, sg_ln_b, sg_w, sg_b, pool_w, pool_scale, w_gate,
           b_gate, w_branch, w_out, ln1_g, ln1_b, w_router_group, b_router_group, w_router_expert,
           b_router_expert, w_exp_gate, w_exp_up, w_exp_down, ln2_g, ln2_b):
    batch, seq, d = x.shape
    assert d == D_MODEL and seq % (ATT_DIL[-1] * ATT_QT) == 0
    p = _prepare_params(w_in, ml_conv_w, ml_gate_b, ml_norm_w, sg_ln_g, sg_ln_b, sg_w, sg_b, pool_w, pool_scale,
                        w_gate, b_gate, w_branch, w_out, ln1_g, ln1_b, w_router_group, b_router_group,
                        w_router_expert, b_router_expert, w_exp_gate, w_exp_up, w_exp_down, ln2_g, ln2_b)
    t = batch * seq
    xf = x.reshape(t, d)
    xb = xf.astype(jnp.bfloat16)
    n_slots = (t * TOP_K // MOE_BLOCK_ROWS + N_EXPERTS) * MOE_BLOCK_ROWS
    xs = jnp.zeros((n_slots, d // 2), jnp.uint32)
    for layer in range(w_in.shape[0]):
        xf, xb = _mixer_layer(xf, xb, p, layer, batch, seq)
        xf, xb, xs = _moe_layer(xf, xb, xs, p, layer)
    return xf.reshape(batch, seq, d)
```

```python
import functools

import numpy as np
import jax
import jax.numpy as jnp
from jax import lax
from jax.experimental import pallas as pl
from jax.experimental.pallas import tpu as pltpu

D_MODEL = 2048
DEPTH = 4
BRANCH_W = 512
N_BRANCH = 4

ML_HEADS = 4
ML_DQK = 64
ML_DV = 128
ML_CHUNK = 128
ML_QK_W = ML_HEADS * ML_DQK
ML_V_W = ML_HEADS * ML_DV
ML_GATES = 2 * 2 * ML_HEADS
ML_W = 2 * ML_QK_W + 2 * ML_V_W

ATT_WINDOW = (128, 512, 2048)
ATT_DIL = (1, 4, 16)
ATT_NGROUP = 3
ATT_HEADS = 8
ATT_DH = 64
ATT_GW = ATT_HEADS * ATT_DH
ATT_W = ATT_NGROUP * ATT_GW
ATT_NEIGH = 64
ATT_QT = 128

SG_CHUNK = 128
SG_GROUPS = 4
POOL_WINDOWS = (2, 4, 8, 16)
POOL_DG = BRANCH_W // len(POOL_WINDOWS)

N_EGROUPS = 4
EXP_PER_GROUP = 8
N_EXPERTS = N_EGROUPS * EXP_PER_GROUP
TOP_K = 2
D_EXPERT = 512
ROUTER_EOFF = 32

ALPHA = (2.0 * DEPTH) ** 0.25
LN_EPS = 1e-5
NEG = -1e30

ROW_DMA_UNROLL = 8
ROW_DMA_GROUP = 64

LANES = 128
VMEM_BYTES_V7X = 64 * 1024 * 1024


def _cparams(semantics, vmem_mb):
    assert vmem_mb * 1024 * 1024 < VMEM_BYTES_V7X
    return pltpu.CompilerParams(dimension_semantics=semantics, vmem_limit_bytes=vmem_mb * 1024 * 1024)


def _sigmoid(x):
    return 1.0 / (1.0 + jnp.exp(-x))


def _bdot(a, b):
    return jnp.dot(a, b, preferred_element_type=jnp.float32)


def _layer_norm(x, g, b):
    mu = jnp.mean(x, axis=-1, keepdims=True)
    xc = x - mu
    var = jnp.mean(xc * xc, axis=-1, keepdims=True)
    return xc * lax.rsqrt(var + LN_EPS) * g + b


def _mm_kernel(x_ref, w_ref, o_ref):
    o_ref[...] = _bdot(x_ref[...], w_ref[...]).astype(o_ref.dtype)


def _matmul(x, w, layer, out_dtype, tm, tn):
    t, k = x.shape
    n = w.shape[2]
    return pl.pallas_call(
        _mm_kernel,
        grid=(t // tm, n // tn),
        in_specs=[pl.BlockSpec((tm, k), lambda i, j: (i, 0)),
                  pl.BlockSpec((None, k, tn), lambda i, j: (layer, 0, j))],
        out_specs=pl.BlockSpec((tm, tn), lambda i, j: (i, j)),
        out_shape=jax.ShapeDtypeStruct((t, n), out_dtype),
        compiler_params=_cparams(("parallel", "arbitrary"), 40),
    )(x, w)


def _split_hi_lo(x):
    hi = x.astype(jnp.bfloat16)
    lo = (x - hi.astype(jnp.float32)).astype(jnp.bfloat16)
    return hi, lo


def _mlstm_kernel(q_ref, k_ref, v_ref, o_ref, g_ref, cw_ref, gb_ref, nw_ref, y_ref,
                  qc_s, kc_s, gp_s, hf_s, hb_s, ct_s):
    s_len = q_ref.shape[0]
    n_chunks = s_len // ML_CHUNK
    lc = ML_CHUNK

    row_c = lax.broadcasted_iota(jnp.int32, (lc, 1), 0)
    cw = cw_ref[...]
    gate_lane = lax.broadcasted_iota(jnp.int32, (lc, LANES), 1)

    def prep(c, carry):
        r0 = pl.multiple_of(c * lc, lc)
        rows = pl.ds(r0, lc)

        def conv_silu(ref, w):
            x = ref[rows, :]
            prev_row = jnp.where(c > 0, ref[pl.ds(jnp.maximum(r0 - 1, 0), 1), :], 0.0)
            next_row = jnp.where(c < n_chunks - 1, ref[pl.ds(jnp.minimum(r0 + lc, s_len - 1), 1), :], 0.0)
            x_prev = jnp.where(row_c == 0, prev_row, pltpu.roll(x, 1, 0))
            x_next = jnp.where(row_c == lc - 1, next_row, pltpu.roll(x, lc - 1, 0))
            y = x_prev * w[0:1] + x * w[1:2] + x_next * w[2:3]
            return y * _sigmoid(y)

        qc_s[rows, :] = (conv_silu(q_ref, cw[:, :ML_QK_W]) * (ML_DQK ** -0.5)).astype(jnp.bfloat16)
        kc_s[rows, :] = conv_silu(k_ref, cw[:, ML_QK_W:]).astype(jnp.bfloat16)

        gp = g_ref[rows, :] + gb_ref[...]
        log_sig = jnp.minimum(gp, 0.0) - jnp.log(1.0 + jnp.exp(-jnp.abs(gp)))
        gp_s[rows, :] = jnp.where((gate_lane % 8) >= ML_HEADS, log_sig, gp)
        return carry

    lax.fori_loop(0, n_chunks, prep, 0)

    r_i = lax.broadcasted_iota(jnp.int32, (lc, lc), 0)
    c_i = lax.broadcasted_iota(jnp.int32, (lc, lc), 1)
    lower = c_i <= r_i
    upper = c_i >= r_i
    tril = jnp.where(lower, 1.0, 0.0).astype(jnp.bfloat16)
    triu = jnp.where(upper, 1.0, 0.0).astype(jnp.bfloat16)
    ones_ext = jnp.ones((lc, ML_DV), jnp.bfloat16)
    sel_r = lax.broadcasted_iota(jnp.int32, (LANES, LANES), 0)
    col_sel = [jnp.where(sel_r == j, 1.0, 0.0).astype(jnp.bfloat16) for j in range(ML_GATES)]

    even_lanes = lax.broadcasted_iota(jnp.int32, (1, LANES), 1) < ML_DQK
    even_rows = lax.broadcasted_iota(jnp.int32, (LANES, 1), 0) < ML_DQK

    ct_s[...] = jnp.zeros(ct_s.shape, ct_s.dtype)

    def body(c, carry):
        units = []
        for direction, h_out in ((0, hf_s), (1, hb_s)):
            cc = c if direction == 0 else n_chunks - 1 - c
            rows = pl.ds(pl.multiple_of(cc * lc, lc), lc)
            gc = gp_s[rows, :]
            gct = gc.T
            gc_hi, gc_lo = _split_hi_lo(gc)
            gct_hi, gct_lo = _split_hi_lo(gct)
            if direction == 0:
                cum_col = _bdot(tril, gc_hi) + _bdot(tril, gc_lo)
                cum_row = _bdot(gct_hi, triu) + _bdot(gct_lo, triu)
                mask = lower
            else:
                cum_col = _bdot(triu, gc_hi) + _bdot(triu, gc_lo)
                cum_row = _bdot(gct_hi, tril) + _bdot(gct_lo, tril)
                mask = upper
            qch = qc_s[rows, :]
            kch = kc_s[rows, :]
            vch = v_ref[rows, :].astype(jnp.bfloat16)
            cum_hi, cum_lo = _split_hi_lo(cum_col)
            for h in range(ML_HEADS):
                slab = slice((h // 2) * LANES, (h // 2 + 1) * LANES)
                own = even_lanes if h % 2 == 0 else jnp.logical_not(even_lanes)
                q_own = jnp.where(own, qch[:, slab], jnp.zeros((lc, LANES), jnp.bfloat16))
                idx = direction * ML_HEADS + h
                sel_i, sel_f = col_sel[direction * 8 + h], col_sel[direction * 8 + ML_HEADS + h]
                units.append(dict(
                    direction=direction, h=h, idx=idx, rows=rows, h_out=h_out, mask=mask, gct=gct,
                    cum_row=cum_row, kch=kch,
                    b_rep=_bdot(cum_hi, sel_f) + _bdot(cum_lo, sel_f),
                    i_rep=_bdot(gc_hi, sel_i) + _bdot(gc_lo, sel_i),
                    v_ext=jnp.concatenate([vch[:, h * ML_DV:(h + 1) * ML_DV], ones_ext], axis=1),
                    qk=lax.dot_general(q_own, kch[:, slab], (((1,), (1,)), ((), ())),
                                       preferred_element_type=jnp.float32),
                    qct=_bdot(q_own, ct_s[idx].astype(jnp.bfloat16))))

        for u in units:
            ci = u["direction"] * 8 + u["h"]
            cf = ci + ML_HEADS
            m = carry[u["idx"]]
            b_rep, i_rep = u["b_rep"], u["i_rep"]
            b_row = u["cum_row"][cf:cf + 1, :]
            i_row = u["gct"][ci:ci + 1, :]
            total = b_row[:, lc - 1:lc] if u["direction"] == 0 else b_row[:, 0:1]
            dmat = jnp.where(u["mask"], b_rep - b_row + i_row, NEG)
            row_max = jnp.broadcast_to(jnp.max(dmat, axis=1, keepdims=True), (lc, LANES))
            m_inter = m + b_rep
            m_t = jnp.maximum(m_inter, row_max)
            u["w_inter"] = jnp.exp(m_inter - m_t)
            u["floor"] = jnp.exp(-m_t)
            u["s"] = (u["qk"] * jnp.exp(dmat - m_t)).astype(jnp.bfloat16)
            g_row = total - b_row + i_row
            m_new = jnp.maximum(m + total, jnp.max(g_row, axis=1, keepdims=True))
            u["decay"] = jnp.exp(m + total - m_new)
            u["wk"] = jnp.exp(total - b_rep + i_rep - m_new)
            u["m_new"] = m_new

        for u in units:
            sv = _bdot(u["s"], u["v_ext"])
            num = u["w_inter"] * u["qct"][:, :ML_DV] + sv[:, :ML_DV]
            den = u["w_inter"] * u["qct"][:, ML_DV:] + sv[:, ML_DV:]
            u["h_out"][u["rows"], u["h"] * ML_DV:(u["h"] + 1) * ML_DV] = num / jnp.maximum(jnp.abs(den), u["floor"])

        for direction in (0, 1):
            us = units[direction * ML_HEADS:(direction + 1) * ML_HEADS]
            kf = us[0]["kch"].astype(jnp.float32)
            for pair in range(ML_HEADS // 2):
                wk_pair = jnp.where(even_lanes, us[2 * pair]["wk"], us[2 * pair + 1]["wk"])
                kwt = (kf[:, pair * LANES:(pair + 1) * LANES] * wk_pair).T.astype(jnp.bfloat16)
                for u in us[2 * pair:2 * pair + 2]:
                    own = even_rows if u["h"] % 2 == 0 else jnp.logical_not(even_rows)
                    kw_own = jnp.where(own, kwt, jnp.zeros((LANES, lc), jnp.bfloat16))
                    ct_s[u["idx"]] = u["decay"] * ct_s[u["idx"]] + _bdot(kw_own, u["v_ext"])
        return tuple(u["m_new"] for u in units)

    zero = jnp.zeros((1, 1), jnp.float32)
    lax.fori_loop(0, n_chunks, body, (zero,) * (2 * ML_HEADS))

    nw = nw_ref[...]

    def finish(c, carry):
        rows = pl.ds(pl.multiple_of(c * lc, lc), lc)
        hsum = hf_s[rows, :] + hb_s[rows, :]
        og = _sigmoid(o_ref[rows, :])
        for h in range(ML_HEADS):
            sl = slice(h * ML_DV, (h + 1) * ML_DV)
            hh = hsum[:, sl]
            mu = jnp.mean(hh, axis=-1, keepdims=True)
            hc = hh - mu
            var = jnp.mean(hc * hc, axis=-1, keepdims=True)
            y_ref[rows, sl] = (hc * lax.rsqrt(var + LN_EPS) * nw[:, sl] * og[:, sl]).astype(y_ref.dtype)
        return carry

    lax.fori_loop(0, n_chunks, finish, 0)


def _mlstm(z_ml, z_g, conv_w, gate_b, norm_w, layer, batch, seq):
    return pl.pallas_call(
        _mlstm_kernel,
        grid=(batch,),
        in_specs=[pl.BlockSpec((None, seq, ML_QK_W), lambda b: (b, 0, 0)),
                  pl.BlockSpec((None, seq, ML_QK_W), lambda b: (b, 0, 1)),
                  pl.BlockSpec((None, seq, ML_V_W), lambda b: (b, 0, 1)),
                  pl.BlockSpec((None, seq, ML_V_W), lambda b: (b, 0, 2)),
                  pl.BlockSpec((None, seq, LANES), lambda b: (b, 0, 0)),
                  pl.BlockSpec((None, 3, 2 * ML_QK_W), lambda b: (layer, 0, 0)),
                  pl.BlockSpec((None, 1, LANES), lambda b: (layer, 0, 0)),
                  pl.BlockSpec((None, 1, ML_V_W), lambda b: (layer, 0, 0))],
        out_specs=pl.BlockSpec((None, seq, ML_V_W), lambda b: (b, 0, 0)),
        out_shape=jax.ShapeDtypeStruct((batch, seq, ML_V_W), jnp.bfloat16),
        scratch_shapes=[pltpu.VMEM((seq, ML_QK_W), jnp.bfloat16),
                        pltpu.VMEM((seq, ML_QK_W), jnp.bfloat16),
                        pltpu.VMEM((seq, LANES), jnp.float32),
                        pltpu.VMEM((seq, ML_V_W), jnp.float32),
                        pltpu.VMEM((seq, ML_V_W), jnp.float32),
                        pltpu.VMEM((2 * ML_HEADS, 2 * ML_DQK, 2 * ML_DV), jnp.float32)],
        compiler_params=_cparams(("parallel",), 56),
    )(z_ml, z_ml, z_ml, z_ml, z_g, conv_w, gate_b, norm_w)


def _alibi_slopes():
    n = ATT_NGROUP * ATT_HEADS
    s = 2.0 ** (-8.0 * np.arange(1, n + 1) / n)
    return s.reshape(ATT_NGROUP, ATT_HEADS).astype(np.float32)


ATT_SLABS = ATT_GW // LANES
ATT_STAGE_ROWS = 256


def _attn_group(q_ref, k_ref, v_ref, y_ref, stage_s, qs, ks, vs, oacc, lacc, osub, lsub, mix_s, bias_s, *,
                group, seq, first, last, prev_dil):
    dil = ATT_DIL[group]
    assert ATT_WINDOW[group] // (2 * dil) == ATT_NEIGH and (not last or dil == 1)
    assert prev_dil is None or (prev_dil % dil == 0 and ATT_QT % (prev_dil // dil) == 0)
    sub_len = seq // dil
    slopes = [float(s) for s in _alibi_slopes()[group] * np.float32(dil)]
    kw = min(2 * ATT_QT, sub_len)
    tiles_per_sub = sub_len // ATT_QT

    if dil == 1:
        q_src, k_src, v_src = q_ref, k_ref, v_ref
    else:
        for src, dst in ((q_ref, qs), (k_ref, ks), (v_ref, vs)):
            def stage(c, carry, src=src):
                rows = pl.ds(pl.multiple_of(c * ATT_STAGE_ROWS, ATT_STAGE_ROWS), ATT_STAGE_ROWS)
                x = src[rows, :].astype(jnp.float32)
                for sl in range(ATT_SLABS):
                    stage_s[sl, rows, :] = x[:, sl * LANES:(sl + 1) * LANES]
                return carry

            lax.fori_loop(0, seq // ATT_STAGE_ROWS, stage, 0)

            def gather_residue(r, carry, dst=dst):
                out_rows = pl.ds(pl.multiple_of(r * sub_len, sub_len), sub_len)
                for sl in range(ATT_SLABS):
                    x = stage_s[sl, pl.ds(r, sub_len, stride=dil), :]
                    dst[out_rows, sl * LANES:(sl + 1) * LANES] = x.astype(jnp.bfloat16)
                return carry

            lax.fori_loop(0, dil, gather_residue, 0)
        q_src, k_src, v_src = qs, ks, vs

    low_half = lax.broadcasted_iota(jnp.int32, (1, LANES), 1) < ATT_DH

    n_offsets = 1 if tiles_per_sub == 1 else 3
    rel0 = (lax.broadcasted_iota(jnp.int32, (ATT_QT, kw), 1) - lax.broadcasted_iota(jnp.int32, (ATT_QT, kw), 0))
    for oi in range(n_offsets):
        arel = jnp.abs(rel0 - oi * ATT_NEIGH).astype(jnp.float32)
        for h in range(ATT_HEADS):
            bias_s[oi * ATT_HEADS + h, :, 0:kw] = jnp.where(arel <= float(ATT_NEIGH), -(slopes[h] * arel), NEG)

    def tile(idx, carry):
        r = idx // tiles_per_sub
        q0 = (idx % tiles_per_sub) * ATT_QT
        k0 = jnp.clip(q0 - ATT_NEIGH, 0, sub_len - kw)
        base = r * sub_len
        bias_base = (q0 - k0) // ATT_NEIGH * ATT_HEADS
        q = q_src[pl.ds(pl.multiple_of(base + q0, ATT_QT), ATT_QT), :] * (ATT_DH ** -0.5)
        k = k_src[pl.ds(pl.multiple_of(base + k0, ATT_NEIGH), kw), :]
        v = v_src[pl.ds(pl.multiple_of(base + k0, ATT_NEIGH), kw), :]
        start = q0 * dil + r
        nat_rows = pl.ds(start, ATT_QT) if dil == 1 else pl.ds(start, ATT_QT, stride=dil)
        scores = []
        for sl in range(ATT_SLABS):
            cols = slice(sl * LANES, (sl + 1) * LANES)
            q2, k2 = q[:, cols], k[:, cols]
            for half in range(2):
                own = low_half if half == 0 else jnp.logical_not(low_half)
                scores.append(lax.dot_general(jnp.where(own, q2, jnp.zeros_like(q2)), k2,
                                              (((1,), (1,)), ((), ())), preferred_element_type=jnp.float32))
        probs = []
        for h in range(ATT_HEADS):
            s = scores[h] + bias_s[bias_base + h, :, 0:kw]
            m = jnp.max(s, axis=1, keepdims=True)
            p = jnp.exp(s - m)
            den = jnp.sum(p, axis=1, keepdims=True)
            probs.append((p.astype(jnp.bfloat16), den, m + jnp.log(den)))
        for sl in range(ATT_SLABS):
            cols = slice(sl * LANES, (sl + 1) * LANES)
            v2 = v[:, cols]
            (p0, den0, lse0), (p1, den1, lse1) = probs[2 * sl], probs[2 * sl + 1]
            o_sl = jnp.where(low_half, _bdot(p0, v2) / den0, _bdot(p1, v2) / den1)
            l_sl = jnp.where(low_half, lse0, lse1)
            if first:
                sub_rows = pl.ds(pl.multiple_of(base + q0, ATT_QT), ATT_QT)
                osub[sl, sub_rows, :] = o_sl
                lsub[sl, sub_rows, :] = l_sl
                continue
            if prev_dil is not None:
                ratio = prev_dil // dil
                seg = ATT_QT // ratio
                for c in range(ratio):
                    src_rows = pl.ds(pl.multiple_of((dil * c + r) * (seq // prev_dil) + q0 // ratio, seg), seg)
                    mix_s[0, pl.ds(c, seg, stride=ratio), :] = osub[sl, src_rows, :]
                    mix_s[1, pl.ds(c, seg, stride=ratio), :] = lsub[sl, src_rows, :]
                o_old, l_old = mix_s[0], mix_s[1]
            else:
                o_old, l_old = oacc[sl, nat_rows, :], lacc[sl, nat_rows, :]
            mx = jnp.maximum(l_old, l_sl)
            w_old, w_new = jnp.exp(l_old - mx), jnp.exp(l_sl - mx)
            tot = w_old + w_new
            o_sl = (w_old * o_old + w_new * o_sl) / tot
            l_sl = mx + jnp.log(tot)
            if last:
                y_ref[pl.ds(pl.multiple_of(start, ATT_QT), ATT_QT), cols] = o_sl.astype(y_ref.dtype)
            else:
                oacc[sl, nat_rows, :] = o_sl
                lacc[sl, nat_rows, :] = l_sl
        return carry

    lax.fori_loop(0, seq // ATT_QT, tile, 0)


def _attn_kernel(q_ref, k_ref, v_ref, y_ref, *scratch, seq):
    gi = pl.program_id(1)
    for step in range(ATT_NGROUP):
        @pl.when(gi == step)
        def _(step=step):
            group = ATT_NGROUP - 1 - step
            _attn_group(q_ref, k_ref, v_ref, y_ref, *scratch, group=group, seq=seq,
                        first=step == 0, last=step == ATT_NGROUP - 1,
                        prev_dil=ATT_DIL[group + 1] if step == 1 else None)


def _dilated_attention(z_at, batch, seq):
    def in_spec(which):
        return pl.BlockSpec((None, seq, ATT_GW),
                            lambda b, gi: (b, 0, which * ATT_NGROUP + ATT_NGROUP - 1 - gi))

    slab = pltpu.VMEM((ATT_SLABS, seq, LANES), jnp.float32)
    sub = pltpu.VMEM((seq, ATT_GW), jnp.bfloat16)
    return pl.pallas_call(
        functools.partial(_attn_kernel, seq=seq),
        grid=(batch, ATT_NGROUP),
        in_specs=[in_spec(0), in_spec(1), in_spec(2)],
        out_specs=pl.BlockSpec((None, seq, ATT_GW), lambda b, gi: (b, 0, 0)),
        out_shape=jax.ShapeDtypeStruct((batch, seq, ATT_GW), jnp.bfloat16),
        scratch_shapes=[slab, sub, sub, sub, slab, slab, slab, slab,
                        pltpu.VMEM((2, ATT_QT, LANES), jnp.float32),
                        pltpu.VMEM((3 * ATT_HEADS, ATT_QT, 2 * ATT_QT), jnp.float32)],
        compiler_params=_cparams(("parallel", "arbitrary"), 56),
    )(z_at, z_at, z_at)


def _gelu_tanh(x):
    return x * (0.5 * (1.0 + jnp.tanh(np.sqrt(2.0 / np.pi).astype(np.float32) * (x + 0.044715 * (x * x * x)))))


def _sg_kernel(u_ref, v_ref, g_ref, b_ref, ws_ref, bs_ref, y_ref):
    tm = u_ref.shape[0]
    dg = BRANCH_W // SG_GROUPS
    u = _gelu_tanh(u_ref[...])
    vn = _layer_norm(_gelu_tanh(v_ref[...]), g_ref[...], b_ref[...]).astype(jnp.bfloat16)
    bs = bs_ref[...]
    for c in range(tm // SG_CHUNK):
        rows = slice(c * SG_CHUNK, (c + 1) * SG_CHUNK)
        for g in range(SG_GROUPS):
            cols = slice(g * dg, (g + 1) * dg)
            mixed = _bdot(ws_ref[g], vn[rows, cols]) + bs[:, cols]
            y_ref[rows, cols] = (u[rows, cols] * mixed).astype(y_ref.dtype)


def _spatial_gating(z_sp, ln_g, ln_b, w_s, b_s_exp, layer, tm):
    t = z_sp.shape[0]
    return pl.pallas_call(
        _sg_kernel,
        grid=(t // tm,),
        in_specs=[pl.BlockSpec((tm, BRANCH_W), lambda i: (i, 0)),
                  pl.BlockSpec((tm, BRANCH_W), lambda i: (i, 1)),
                  pl.BlockSpec((None, 1, BRANCH_W), lambda i: (layer, 0, 0)),
                  pl.BlockSpec((None, 1, BRANCH_W), lambda i: (layer, 0, 0)),
                  pl.BlockSpec((None, SG_GROUPS, SG_CHUNK, SG_CHUNK), lambda i: (layer, 0, 0, 0)),
                  pl.BlockSpec((None, SG_CHUNK, BRANCH_W), lambda i: (layer, 0, 0))],
        out_specs=pl.BlockSpec((tm, BRANCH_W), lambda i: (i, 0)),
        out_shape=jax.ShapeDtypeStruct((t, BRANCH_W), jnp.bfloat16),
        compiler_params=_cparams(("parallel",), 32),
    )(z_sp, z_sp, ln_g, ln_b, w_s, b_s_exp)


def _pool_kernel(p_ref, w_ref, sc_ref, y_ref):
    s_len = p_ref.shape[0]
    row = lax.broadcasted_iota(jnp.int32, (s_len, 1), 0)
    sc = sc_ref[...]
    for g, win in enumerate(POOL_WINDOWS):
        cols = slice(g * POOL_DG, (g + 1) * POOL_DG)
        half = win // 2
        p = p_ref[:, cols]
        acc = jnp.zeros_like(p)
        for j in range(-half, half):
            if j == 0:
                acc = acc + p
            else:
                shifted = pltpu.roll(p, (-j) % s_len, 0)
                ok = (row + j >= 0) & (row + j < s_len)
                acc = acc + jnp.where(ok, shifted, 0.0)
        cnt = (jnp.minimum(row + half, s_len) - jnp.maximum(row - half, 0)).astype(jnp.float32)
        d = acc / cnt - p
        y = _bdot(d.astype(jnp.bfloat16), w_ref[g]) * sc[:, cols]
        y_ref[:, cols] = y.astype(y_ref.dtype)


def _multiscale_pool(z_sp, pool_w, pool_scale, layer, batch, seq):
    return pl.pallas_call(
        _pool_kernel,
        grid=(batch,),
        in_specs=[pl.BlockSpec((None, seq, BRANCH_W), lambda b: (b, 0, 2)),
                  pl.BlockSpec((None, len(POOL_WINDOWS), POOL_DG, POOL_DG), lambda b: (layer, 0, 0, 0)),
                  pl.BlockSpec((None, 1, BRANCH_W), lambda b: (layer, 0, 0))],
        out_specs=pl.BlockSpec((None, seq, BRANCH_W), lambda b: (b, 0, 0)),
        out_shape=jax.ShapeDtypeStruct((batch, seq, BRANCH_W), jnp.bfloat16),
        compiler_params=_cparams(("parallel",), 48),
    )(z_sp, pool_w, pool_scale)


def _gate_mix_kernel(xb_ref, y0, y1, y2, y3, wg0, wg1, wg2, wg3, bg0, bg1, bg2, bg3, wb_ref, o_ref):
    xb = xb_ref[...]
    merged = None
    for n, (y, wg, bg) in enumerate(((y0, wg0, bg0), (y1, wg1, bg1), (y2, wg2, bg2), (y3, wg3, bg3))):
        gate = _sigmoid(_bdot(xb, wg[...]) + bg[...])
        term = gate * _bdot(y[...], wb_ref[n])
        merged = term if merged is None else merged + term
    o_ref[...] = merged.astype(o_ref.dtype)


def _gate_mix(xb, ys, w_gate, b_gate, w_branch, layer, tm, tn):
    t = xb.shape[0]
    nj = D_MODEL // tn
    row_spec = lambda w: pl.BlockSpec((tm, w), lambda i, j: (i, 0))
    wg_specs = [pl.BlockSpec((None, D_MODEL, tn), lambda i, j, n=n: (layer, 0, n * nj + j)) for n in range(N_BRANCH)]
    bg_specs = [pl.BlockSpec((None, 1, tn), lambda i, j, n=n: (layer, 0, n * nj + j)) for n in range(N_BRANCH)]
    return pl.pallas_call(
        _gate_mix_kernel,
        grid=(t // tm, nj),
        in_specs=[row_spec(D_MODEL)] + [row_spec(BRANCH_W)] * N_BRANCH + wg_specs + bg_specs
                 + [pl.BlockSpec((None, N_BRANCH, BRANCH_W, tn), lambda i, j: (layer, 0, 0, j))],
        out_specs=pl.BlockSpec((tm, tn), lambda i, j: (i, j)),
        out_shape=jax.ShapeDtypeStruct((t, D_MODEL), jnp.bfloat16),
        compiler_params=_cparams(("parallel", "arbitrary"), 48),
    )(xb, *ys, *([w_gate] * N_BRANCH), *([b_gate] * N_BRANCH), w_branch)


def _out_proj_ln_kernel(m_ref, wo_ref, xf_ref, g_ref, b_ref, of_ref, ob_ref):
    out = _layer_norm(ALPHA * xf_ref[...] + _bdot(m_ref[...], wo_ref[...]), g_ref[...], b_ref[...])
    of_ref[...] = out
    ob_ref[...] = out.astype(ob_ref.dtype)


def _out_proj_ln(merged, w_out, xf, ln_g, ln_b, layer, tm):
    t = xf.shape[0]
    row_spec = pl.BlockSpec((tm, D_MODEL), lambda i: (i, 0))
    vec_spec = pl.BlockSpec((None, 1, D_MODEL), lambda i: (layer, 0, 0))
    return pl.pallas_call(
        _out_proj_ln_kernel,
        grid=(t // tm,),
        in_specs=[row_spec, pl.BlockSpec((None, D_MODEL, D_MODEL), lambda i: (layer, 0, 0)),
                  row_spec, vec_spec, vec_spec],
        out_specs=[row_spec, row_spec],
        out_shape=[jax.ShapeDtypeStruct((t, D_MODEL), jnp.float32),
                   jax.ShapeDtypeStruct((t, D_MODEL), jnp.bfloat16)],
        compiler_params=_cparams(("parallel",), 48),
    )(merged, w_out, xf, ln_g, ln_b)


def _router_kernel(x_ref, w_ref, b_ref, id_ref, wt_ref, cnt_ref):
    x_hi, x_lo = _split_hi_lo(x_ref[...])
    w_hi, w_lo = w_ref[0], w_ref[1]
    logits = _bdot(x_hi, w_hi) + _bdot(x_lo, w_hi) + _bdot(x_hi, w_lo) + b_ref[...]
    lane = lax.broadcasted_iota(jnp.int32, logits.shape, 1)
    lane_f = lane.astype(jnp.float32)
    no_lane = float(LANES)

    def first_argmax(vals):
        mx = jnp.max(vals, axis=1, keepdims=True)
        idx = jnp.min(jnp.where(vals == mx, lane_f, no_lane), axis=1, keepdims=True)
        return mx, idx.astype(jnp.int32)

    is_group = lane < N_EGROUPS
    g_max, g_sel = first_argmax(jnp.where(is_group, logits, NEG))
    p_sel = 1.0 / jnp.sum(jnp.where(is_group, jnp.exp(logits - g_max), 0.0), axis=1, keepdims=True)

    lo = ROUTER_EOFF + g_sel * EXP_PER_GROUP
    cand = jnp.where((lane >= lo) & (lane < lo + EXP_PER_GROUP), logits, NEG)
    l1, i1 = first_argmax(cand)
    l2, i2 = first_argmax(jnp.where(lane == i1, NEG, cand))
    e = jnp.exp(l2 - l1)
    w1 = p_sel / (1.0 + e)
    w2 = p_sel * e / (1.0 + e)
    e1, e2 = i1 - ROUTER_EOFF, i2 - ROUTER_EOFF

    tm = logits.shape[0]
    hit1, hit2 = lane == e1, lane == e2
    hits = jnp.where(jnp.logical_or(hit1, hit2), 1.0, 0.0)
    earlier = (lax.broadcasted_iota(jnp.int32, (tm, tm), 1) < lax.broadcasted_iota(jnp.int32, (tm, tm), 0))
    before = _bdot(jnp.where(earlier, 1.0, 0.0).astype(jnp.bfloat16), hits.astype(jnp.bfloat16))
    rank1 = jnp.sum(jnp.where(hit1, before, 0.0), axis=1, keepdims=True).astype(jnp.int32)
    rank2 = jnp.sum(jnp.where(hit2, before, 0.0), axis=1, keepdims=True).astype(jnp.int32)
    counts = jnp.sum(hits, axis=0, keepdims=True).astype(jnp.int32)

    id_ref[...] = jnp.where(lane == 0, e1, jnp.where(lane == 1, e2,
                            jnp.where(lane == 2, rank1, jnp.where(lane == 3, rank2, 0))))
    wt_ref[...] = jnp.where(lane == 0, w1, jnp.where(lane == 1, w2, 0.0))
    cnt_ref[...] = jnp.broadcast_to(counts, cnt_ref.shape)


def _router(xf, w_r, b_r, layer, tm):
    t = xf.shape[0]
    out_spec = pl.BlockSpec((tm, LANES), lambda i: (i, 0))
    return pl.pallas_call(
        _router_kernel,
        grid=(t // tm,),
        in_specs=[pl.BlockSpec((tm, D_MODEL), lambda i: (i, 0)),
                  pl.BlockSpec((None, 2, D_MODEL, LANES), lambda i: (layer, 0, 0, 0)),
                  pl.BlockSpec((None, 1, LANES), lambda i: (layer, 0, 0))],
        out_specs=[out_spec, out_spec, pl.BlockSpec((None, 8, LANES), lambda i: (i, 0, 0))],
        out_shape=[jax.ShapeDtypeStruct((t, LANES), jnp.int32),
                   jax.ShapeDtypeStruct((t, LANES), jnp.float32),
                   jax.ShapeDtypeStruct((t // tm, 8, LANES), jnp.int32)],
        compiler_params=_cparams(("parallel",), 32),
    )(xf, w_r, b_r)


def _dispatch_plan(info, counts, blk, tm):
    t = info.shape[0]
    c = counts[:, 0, :N_EXPERTS]
    tile_off = jnp.cumsum(c, axis=0) - c
    total = jnp.sum(c, axis=0)
    padded = (total + blk - 1) // blk * blk
    pad_end = jnp.cumsum(padded)
    base = (pad_end - padded)[None, :] + tile_off
    base_tok = jnp.repeat(base, tm, axis=0)
    hit = info[:, :TOP_K, None] == jnp.arange(N_EXPERTS, dtype=jnp.int32)[None, None, :]
    dest = jnp.sum(jnp.where(hit, base_tok[:, None, :], 0), axis=2) + info[:, TOP_K:2 * TOP_K]
    nb = t * TOP_K // blk + N_EXPERTS
    blk_start = jnp.arange(nb, dtype=jnp.int32) * blk
    blk_e = jnp.minimum(jnp.sum((pad_end[None, :] <= blk_start[:, None]).astype(jnp.int32), axis=1),
                        N_EXPERTS - 1)
    n_used = (pad_end[-1] // blk).astype(jnp.int32).reshape(1)
    return dest.astype(jnp.int32), blk_e, n_used


def _dispatch_kernel(dest_ref, dest_prev_ref, x_ref, xs_in, xs_out, pk, sem):
    del xs_in
    i = pl.program_id(0)
    tm = x_ref.shape[0]
    half = D_MODEL // 2
    slot = lax.rem(i, 2)
    grp = ROW_DMA_GROUP

    def copy(idx_ref, s, r, j):
        return pltpu.make_async_copy(pk.at[s, pl.ds(r, 1), :], xs_out.at[pl.ds(idx_ref[0, j], 1), :], sem.at[s])

    def pack(t):
        rows = pl.ds(pl.multiple_of(t * grp, grp), grp)
        x = x_ref[rows, :].astype(jnp.float32)
        hi = lax.bitcast_convert_type(x[:, :half], jnp.uint32)
        lo = lax.bitcast_convert_type(x[:, half:], jnp.uint32)
        pk[slot, rows, :] = hi | (lo >> 16)

    def issue(t):
        for u in range(grp):
            for k in range(TOP_K):
                copy(dest_ref, slot, t * grp + u, k * tm + t * grp + u).start(priority=k)

    def group(t, carry):
        pack(t + 1)
        issue(t)
        return carry

    n_groups = tm // grp
    pack(0)
    lax.fori_loop(0, n_groups - 1, group, 0)
    issue(n_groups - 1)

    def wait_prev(j, carry):
        copy(dest_prev_ref, 1 - slot, lax.rem(j, tm), j).wait()
        return carry

    @pl.when(i > 0)
    def _():
        lax.fori_loop(0, TOP_K * tm, wait_prev, 0, unroll=8)

    def wait_own(j, carry):
        copy(dest_ref, slot, lax.rem(j, tm), j).wait()
        return carry

    @pl.when(i == pl.num_programs(0) - 1)
    def _():
        lax.fori_loop(0, TOP_K * tm, wait_own, 0, unroll=8)


def _dest_tiles(dest, tm):
    nt = dest.shape[0] // tm
    return dest.reshape(nt, tm, TOP_K).transpose(0, 2, 1).reshape(nt, 1, TOP_K * tm)


def _dispatch_rows(xb, dest_tiles, xs_prev):
    t = xb.shape[0]
    nt, _, two_tm = dest_tiles.shape
    tm = two_tm // TOP_K
    return pl.pallas_call(
        _dispatch_kernel,
        grid=(nt,),
        in_specs=[pl.BlockSpec((None, 1, two_tm), lambda i: (i, 0, 0), memory_space=pltpu.SMEM),
                  pl.BlockSpec((None, 1, two_tm), lambda i: (jnp.maximum(i - 1, 0), 0, 0),
                               memory_space=pltpu.SMEM),
                  pl.BlockSpec((tm, D_MODEL), lambda i: (i, 0)),
                  pl.BlockSpec(memory_space=pl.ANY)],
        out_specs=pl.BlockSpec(memory_space=pl.ANY),
        out_shape=jax.ShapeDtypeStruct(xs_prev.shape, xs_prev.dtype),
        scratch_shapes=[pltpu.VMEM((2, tm, D_MODEL // 2), jnp.uint32),
                        pltpu.SemaphoreType.DMA((2,))],
        input_output_aliases={3: 0},
        compiler_params=_cparams(("arbitrary",), 32),
    )(dest_tiles, dest_tiles, xb, xs_prev)


def _row_copy(src_hbm, idx, buf, slot, r, sem):
    return pltpu.make_async_copy(src_hbm.at[pl.ds(idx, 1), :], buf.at[slot, pl.ds(r, 1), :], sem.at[slot])


def _start_row_gather(src_hbm, idx_ref, buf, slot, sem, n_rows):
    def body(g, carry):
        for u in range(ROW_DMA_UNROLL):
            r = g * ROW_DMA_UNROLL + u
            _row_copy(src_hbm, idx_ref[0, r], buf, slot, r, sem).start(priority=u % 2)
        return carry
    lax.fori_loop(0, n_rows // ROW_DMA_UNROLL, body, 0)


def _wait_row_gather(src_hbm, idx_ref, buf, slot, sem, n_rows):
    def body(r, carry):
        _row_copy(src_hbm, idx_ref[0, r], buf, slot, r, sem).wait()
        return carry
    lax.fori_loop(0, n_rows, body, 0, unroll=8)


def _expert_kernel(blk_e_ref, n_used_ref, xs_ref, w1_ref, w3_ref, w2_ref, y_ref, w1_s, w3_s, w2_s):
    i = pl.program_id(0)
    n_used = n_used_ref[0]
    new_expert = jnp.logical_or(i == 0, blk_e_ref[i] != blk_e_ref[jnp.maximum(i - 1, 0)])

    @pl.when(jnp.logical_and(i < n_used, new_expert))
    def _():
        w1_s[...] = w1_ref[...].astype(jnp.bfloat16)
        w3_s[...] = w3_ref[...].astype(jnp.bfloat16)
        w2_s[...] = w2_ref[...].astype(jnp.bfloat16)

    @pl.when(i < n_used)
    def _():
        packed = xs_ref[...]
        x_a = lax.bitcast_convert_type(packed & jnp.uint32(0xFFFF0000), jnp.float32)
        x_b = lax.bitcast_convert_type(packed << 16, jnp.float32)
        xb = jnp.concatenate([x_a, x_b], axis=1).astype(jnp.bfloat16)
        a = _bdot(xb, w1_s[...])
        h = (a * _sigmoid(a)) * _bdot(xb, w3_s[...])
        y_ref[...] = _bdot(h.astype(jnp.bfloat16), w2_s[...])

    @pl.when(i >= n_used)
    def _():
        y_ref[...] = jnp.zeros(y_ref.shape, y_ref.dtype)


def _experts(xs, blk_e, n_used, w1, w3, w2, layer, blk):
    nb = xs.shape[0] // blk

    def used(i, nu):
        return jnp.minimum(i, nu[0] - 1)

    def w_index(i, be, nu):
        return (layer, be[used(i, nu)], 0, 0)

    grid_spec = pltpu.PrefetchScalarGridSpec(
        num_scalar_prefetch=2,
        grid=(nb,),
        in_specs=[pl.BlockSpec((blk, D_MODEL // 2), lambda i, be, nu: (used(i, nu), 0)),
                  pl.BlockSpec((None, None, D_MODEL, D_EXPERT), w_index),
                  pl.BlockSpec((None, None, D_MODEL, D_EXPERT), w_index),
                  pl.BlockSpec((None, None, D_EXPERT, D_MODEL), w_index)],
        out_specs=pl.BlockSpec((blk, D_MODEL), lambda i, be, nu: (i, 0)),
        scratch_shapes=[pltpu.VMEM((D_MODEL, D_EXPERT), jnp.bfloat16),
                        pltpu.VMEM((D_MODEL, D_EXPERT), jnp.bfloat16),
                        pltpu.VMEM((D_EXPERT, D_MODEL), jnp.bfloat16)])
    return pl.pallas_call(
        _expert_kernel,
        grid_spec=grid_spec,
        out_shape=jax.ShapeDtypeStruct((nb * blk, D_MODEL), jnp.float32),
        compiler_params=_cparams(("arbitrary",), 56),
    )(blk_e, n_used, xs, w1, w3, w2)


def _combine_kernel(pos_ref, pos_next_ref, y_hbm, xf_ref, wt_ref, g_ref, b_ref, of_ref, ob_ref, ybuf, sem):
    i = pl.program_id(0)
    tm = xf_ref.shape[0]
    slot = lax.rem(i, 2)

    @pl.when(i == 0)
    def _():
        _start_row_gather(y_hbm, pos_ref, ybuf, 0, sem, TOP_K * tm)

    _wait_row_gather(y_hbm, pos_ref, ybuf, slot, sem, TOP_K * tm)
    gain, shift = g_ref[...], b_ref[...]
    grp = ROW_DMA_GROUP

    def make_group(issue_next):
        def group(t, carry):
            r0 = pl.multiple_of(t * grp, grp)
            rows = pl.ds(r0, grp)
            wt = wt_ref[rows, :]
            moe = wt[:, 0:1] * ybuf[slot, rows, :] + wt[:, 1:2] * ybuf[slot, pl.ds(tm + r0, grp), :]
            out = _layer_norm(ALPHA * xf_ref[rows, :] + moe, gain, shift)
            of_ref[rows, :] = out
            ob_ref[rows, :] = out.astype(ob_ref.dtype)
            if issue_next:
                for u in range(TOP_K * grp):
                    j = t * (TOP_K * grp) + u
                    _row_copy(y_hbm, pos_next_ref[0, j], ybuf, 1 - slot, j, sem).start(priority=u % 2)
            return carry
        return group

    has_next = i + 1 < pl.num_programs(0)

    @pl.when(has_next)
    def _():
        lax.fori_loop(0, tm // grp, make_group(True), 0)

    @pl.when(jnp.logical_not(has_next))
    def _():
        lax.fori_loop(0, tm // grp, make_group(False), 0)


def _combine(y_sorted, pos_tiles, xf, wts, ln_g, ln_b, layer):
    t = xf.shape[0]
    nt = pos_tiles.shape[0]
    tm = t // nt
    row_spec = pl.BlockSpec((tm, D_MODEL), lambda i: (i, 0))
    vec_spec = pl.BlockSpec((None, 1, D_MODEL), lambda i: (layer, 0, 0))
    return pl.pallas_call(
        _combine_kernel,
        grid=(nt,),
        in_specs=[pl.BlockSpec((None, 1, TOP_K * tm), lambda i: (i, 0, 0), memory_space=pltpu.SMEM),
                  pl.BlockSpec((None, 1, TOP_K * tm), lambda i: (jnp.minimum(i + 1, nt - 1), 0, 0),
                               memory_space=pltpu.SMEM),
                  pl.BlockSpec(memory_space=pl.ANY),
                  row_spec,
                  pl.BlockSpec((tm, LANES), lambda i: (i, 0)),
                  vec_spec, vec_spec],
        out_specs=[row_spec, row_spec],
        out_shape=[jax.ShapeDtypeStruct((t, D_MODEL), jnp.float32),
                   jax.ShapeDtypeStruct((t, D_MODEL), jnp.bfloat16)],
        scratch_shapes=[pltpu.VMEM((2, TOP_K * tm, D_MODEL), jnp.float32),
                        pltpu.SemaphoreType.DMA((2,))],
        compiler_params=_cparams(("arbitrary",), 48),
    )(pos_tiles, pos_tiles, y_sorted, xf, wts, ln_g, ln_b)


MOE_BLOCK_ROWS = 256
PROJ_TM, PROJ_TN = 2048, 512
MIX_TM, MIX_TN = 1024, 256
OUT_TM = 512
SG_TM = 512
ROUTER_TM = 512
ROW_DMA_TM = 256


def _mixer_layer(xf, xb, p, layer, batch, seq):
    t = batch * seq
    z_ml = _matmul(xb, p["w_ml"], layer, jnp.float32, PROJ_TM, PROJ_TN)
    z_g = _matmul(xb, p["w_mg"], layer, jnp.float32, PROJ_TM, LANES)
    z_at = _matmul(xb, p["w_at"], layer, jnp.bfloat16, PROJ_TM, PROJ_TN)
    z_sp = _matmul(xb, p["w_sp"], layer, jnp.float32, PROJ_TM, PROJ_TN)
    y_ml = _mlstm(z_ml.reshape(batch, seq, ML_W), z_g.reshape(batch, seq, LANES),
                  p["ml_conv_w"], p["ml_gate_b"], p["ml_norm_w"], layer, batch, seq).reshape(t, BRANCH_W)
    y_at = _dilated_attention(z_at.reshape(batch, seq, 3 * ATT_W), batch, seq).reshape(t, BRANCH_W)
    y_sg = _spatial_gating(z_sp, p["sg_ln_g"], p["sg_ln_b"], p["sg_w"], p["sg_b"], layer, SG_TM)
    y_pl = _multiscale_pool(z_sp.reshape(batch, seq, 3 * BRANCH_W), p["pool_w"], p["pool_scale"],
                            layer, batch, seq).reshape(t, BRANCH_W)
    merged = _gate_mix(xb, (y_ml, y_at, y_sg, y_pl), p["w_gate"], p["b_gate"], p["w_branch"], layer,
                       MIX_TM, MIX_TN)
    return _out_proj_ln(merged, p["w_out"], xf, p["ln1_g"], p["ln1_b"], layer, OUT_TM)


def _moe_layer(xf, xb, xs_prev, p, layer):
    info, wts, counts = _router(xf, p["w_router"], p["b_router"], layer, ROUTER_TM)
    dest, blk_e, n_used = _dispatch_plan(info, counts, MOE_BLOCK_ROWS, ROUTER_TM)
    dest_tiles = _dest_tiles(dest, ROW_DMA_TM)
    xs = _dispatch_rows(xb, dest_tiles, xs_prev)
    y_sorted = _experts(xs, blk_e, n_used, p["w_exp_gate"], p["w_exp_up"], p["w_exp_down"], layer, MOE_BLOCK_ROWS)
    xf, xb = _combine(y_sorted, dest_tiles, xf, wts, p["ln2_g"], p["ln2_b"], layer)
    return xf, xb, xs


def _prepare_params(w_in, ml_conv_w, ml_gate_b, ml_norm_w, sg_ln_g, sg_ln_b, sg_w, sg_b, pool_w, pool_scale,
                    w_gate, b_gate, w_branch, w_out, ln1_g, ln1_b, w_router_group, b_router_group,
                    w_router_expert, b_router_expert, w_exp_gate, w_exp_up, w_exp_down, ln2_g, ln2_b):
    bf = jnp.bfloat16
    depth = w_in.shape[0]
    c_ml, c_g = ML_W, ML_W + ML_GATES
    c_at = c_g + 3 * ATT_W
    row = lambda a: a.reshape(depth, 1, -1)
    lane_pad = lambda a: jnp.pad(a, [(0, 0)] * (a.ndim - 1) + [(0, LANES - a.shape[-1])])
    w_r = jnp.zeros((depth, D_MODEL, LANES), jnp.float32)
    w_r = w_r.at[:, :, :N_EGROUPS].set(w_router_group).at[:, :, ROUTER_EOFF:ROUTER_EOFF + N_EXPERTS].set(w_router_expert)
    b_r = jnp.zeros((depth, LANES), jnp.float32)
    b_r = b_r.at[:, :N_EGROUPS].set(b_router_group).at[:, ROUTER_EOFF:ROUTER_EOFF + N_EXPERTS].set(b_router_expert)
    w_r_hi = w_r.astype(bf)
    w_r_lo = (w_r - w_r_hi.astype(jnp.float32)).astype(bf)
    sg_b_exp = jnp.repeat(jnp.swapaxes(sg_b, 1, 2), BRANCH_W // SG_GROUPS, axis=2)
    return dict(
        w_ml=w_in[:, :, :c_ml].astype(bf),
        w_mg=lane_pad(w_in[:, :, c_ml:c_g]).astype(bf),
        w_at=w_in[:, :, c_g:c_at].astype(bf),
        w_sp=w_in[:, :, c_at:].astype(bf),
        ml_conv_w=ml_conv_w,
        ml_gate_b=lane_pad(ml_gate_b.reshape(depth, 1, ML_GATES)),
        ml_norm_w=row(ml_norm_w),
        sg_ln_g=row(sg_ln_g), sg_ln_b=row(sg_ln_b), sg_w=sg_w.astype(bf), sg_b=sg_b_exp,
        pool_w=pool_w.astype(bf), pool_scale=row(pool_scale),
        w_gate=w_gate.astype(bf), b_gate=row(b_gate), w_branch=w_branch.astype(bf), w_out=w_out.astype(bf),
        ln1_g=row(ln1_g), ln1_b=row(ln1_b),
        w_router=jnp.stack([w_r_hi, w_r_lo], axis=1), b_router=row(b_r),
        w_exp_gate=w_exp_gate, w_exp_up=w_exp_up, w_exp_down=w_exp_down,
        ln2_g=row(ln2_g), ln2_b=row(ln2_b))


def kernel(x, w_in, ml_conv_w, ml_gate_b, ml_norm_w, sg_ln_g, sg_ln_b, sg_w, sg_b, pool_w, pool_scale, w_gate,
           b_gate, w_branch, w_out, ln1_g, ln1_b, w_router_group, b_router_group, w_router_expert,
           b_router_expert, w_exp_gate, w_exp_up, w_exp_down, ln2_g, ln2_b):
    batch, seq, d = x.shape
    assert d == D_MODEL and seq % (ATT_DIL[-1] * ATT_QT) == 0
    p = _prepare_params(w_in, ml_conv_w, ml_gate_b, ml_norm_w, sg_ln_g, sg_ln_b, sg_w, sg_b, pool_w, pool_scale,
                        w_gate, b_gate, w_branch, w_out, ln1_g, ln1_b, w_router_group, b_router_group,
                        w_router_expert, b_router_expert, w_exp_gate, w_exp_up, w_exp_down, ln2_g, ln2_b)
    t = batch * seq
    xf = x.reshape(t, d)
    xb = xf.astype(jnp.bfloat16)
    n_slots = (t * TOP_K // MOE_BLOCK_ROWS + N_EXPERTS) * MOE_BLOCK_ROWS
    xs = jnp.zeros((n_slots, d // 2), jnp.uint32)
    for layer in range(w_in.shape[0]):
        xf, xb = _mixer_layer(xf, xb, p, layer, batch, seq)
        xf, xb, xs = _moe_layer(xf, xb, xs, p, layer)
    return xf.reshape(batch, seq, d)
```

```python
import functools

import numpy as np
import jax
import jax.numpy as jnp
from jax import lax
from jax.experimental import pallas as pl
from jax.experimental.pallas import tpu as pltpu

D_MODEL = 2048
DEPTH = 4
BRANCH_W = 512
N_BRANCH = 4

ML_HEADS = 4
ML_DQK = 64
ML_DV = 128
ML_CHUNK = 128
ML_QK_W = ML_HEADS * ML_DQK
ML_V_W = ML_HEADS * ML_DV
ML_GATES = 2 * 2 * ML_HEADS
ML_W = 2 * ML_QK_W + 2 * ML_V_W

ATT_WINDOW = (128, 512, 2048)
ATT_DIL = (1, 4, 16)
ATT_NGROUP = 3
ATT_HEADS = 8
ATT_DH = 64
ATT_GW = ATT_HEADS * ATT_DH
ATT_W = ATT_NGROUP * ATT_GW
ATT_NEIGH = 64
ATT_QT = 128

SG_CHUNK = 128
SG_GROUPS = 4
POOL_WINDOWS = (2, 4, 8, 16)
POOL_DG = BRANCH_W // len(POOL_WINDOWS)

N_EGROUPS = 4
EXP_PER_GROUP = 8
N_EXPERTS = N_EGROUPS * EXP_PER_GROUP
TOP_K = 2
D_EXPERT = 512
ROUTER_EOFF = 32

ALPHA = (2.0 * DEPTH) ** 0.25
LN_EPS = 1e-5
NEG = -1e30

ROW_DMA_UNROLL = 8
ROW_DMA_GROUP = 64

LANES = 128
VMEM_BYTES_V7X = 64 * 1024 * 1024


def _cparams(semantics, vmem_mb):
    assert vmem_mb * 1024 * 1024 < VMEM_BYTES_V7X
    return pltpu.CompilerParams(dimension_semantics=semantics, vmem_limit_bytes=vmem_mb * 1024 * 1024)


def _sigmoid(x):
    return 1.0 / (1.0 + jnp.exp(-x))


def _bdot(a, b):
    return jnp.dot(a, b, preferred_element_type=jnp.float32)


def _layer_norm(x, g, b):
    mu = jnp.mean(x, axis=-1, keepdims=True)
    xc = x - mu
    var = jnp.mean(xc * xc, axis=-1, keepdims=True)
    return xc * lax.rsqrt(var + LN_EPS) * g + b


def _mm_kernel(x_ref, w_ref, o_ref):
    o_ref[...] = _bdot(x_ref[...], w_ref[...]).astype(o_ref.dtype)


def _matmul(x, w, layer, out_dtype, tm, tn):
    t, k = x.shape
    n = w.shape[2]
    return pl.pallas_call(
        _mm_kernel,
        grid=(t // tm, n // tn),
        in_specs=[pl.BlockSpec((tm, k), lambda i, j: (i, 0)),
                  pl.BlockSpec((None, k, tn), lambda i, j: (layer, 0, j))],
        out_specs=pl.BlockSpec((tm, tn), lambda i, j: (i, j)),
        out_shape=jax.ShapeDtypeStruct((t, n), out_dtype),
        compiler_params=_cparams(("parallel", "arbitrary"), 40),
    )(x, w)


def _split_hi_lo(x):
    hi = x.astype(jnp.bfloat16)
    lo = (x - hi.astype(jnp.float32)).astype(jnp.bfloat16)
    return hi, lo


def _mlstm_kernel(q_ref, k_ref, v_ref, o_ref, g_ref, cw_ref, gb_ref, nw_ref, y_ref,
                  qc_s, kc_s, gp_s, hf_s, hb_s, ct_s):
    s_len = q_ref.shape[0]
    n_chunks = s_len // ML_CHUNK
    lc = ML_CHUNK

    row_c = lax.broadcasted_iota(jnp.int32, (lc, 1), 0)
    cw = cw_ref[...]
    gate_lane = lax.broadcasted_iota(jnp.int32, (lc, LANES), 1)

    def prep(c, carry):
        r0 = pl.multiple_of(c * lc, lc)
        rows = pl.ds(r0, lc)

        def conv_silu(ref, w):
            x = ref[rows, :]
            prev_row = jnp.where(c > 0, ref[pl.ds(jnp.maximum(r0 - 1, 0), 1), :], 0.0)
            next_row = jnp.where(c < n_chunks - 1, ref[pl.ds(jnp.minimum(r0 + lc, s_len - 1), 1), :], 0.0)
            x_prev = jnp.where(row_c == 0, prev_row, pltpu.roll(x, 1, 0))
            x_next = jnp.where(row_c == lc - 1, next_row, pltpu.roll(x, lc - 1, 0))
            y = x_prev * w[0:1] + x * w[1:2] + x_next * w[2:3]
            return y * _sigmoid(y)

        qc_s[rows, :] = (conv_silu(q_ref, cw[:, :ML_QK_W]) * (ML_DQK ** -0.5)).astype(jnp.bfloat16)
        kc_s[rows, :] = conv_silu(k_ref, cw[:, ML_QK_W:]).astype(jnp.bfloat16)

        gp = g_ref[rows, :] + gb_ref[...]
        log_sig = jnp.minimum(gp, 0.0) - jnp.log(1.0 + jnp.exp(-jnp.abs(gp)))
        gp_s[rows, :] = jnp.where((gate_lane % 8) >= ML_HEADS, log_sig, gp)
        return carry

    lax.fori_loop(0, n_chunks, prep, 0)

    r_i = lax.broadcasted_iota(jnp.int32, (lc, lc), 0)
    c_i = lax.broadcasted_iota(jnp.int32, (lc, lc), 1)
    lower = c_i <= r_i
    upper = c_i >= r_i
    tril = jnp.where(lower, 1.0, 0.0).astype(jnp.bfloat16)
    triu = jnp.where(upper, 1.0, 0.0).astype(jnp.bfloat16)
    ones_ext = jnp.ones((lc, ML_DV), jnp.bfloat16)
    sel_r = lax.broadcasted_iota(jnp.int32, (LANES, LANES), 0)
    col_sel = [jnp.where(sel_r == j, 1.0, 0.0).astype(jnp.bfloat16) for j in range(ML_GATES)]

    even_lanes = lax.broadcasted_iota(jnp.int32, (1, LANES), 1) < ML_DQK
    even_rows = lax.broadcasted_iota(jnp.int32, (LANES, 1), 0) < ML_DQK

    ct_s[...] = jnp.zeros(ct_s.shape, ct_s.dtype)

    def body(c, carry):
        units = []
        for direction, h_out in ((0, hf_s), (1, hb_s)):
            cc = c if direction == 0 else n_chunks - 1 - c
            rows = pl.ds(pl.multiple_of(cc * lc, lc), lc)
            gc = gp_s[rows, :]
            gct = gc.T
            gc_hi, gc_lo = _split_hi_lo(gc)
            gct_hi, gct_lo = _split_hi_lo(gct)
            if direction == 0:
                cum_col = _bdot(tril, gc_hi) + _bdot(tril, gc_lo)
                cum_row = _bdot(gct_hi, triu) + _bdot(gct_lo, triu)
                mask = lower
            else:
                cum_col = _bdot(triu, gc_hi) + _bdot(triu, gc_lo)
                cum_row = _bdot(gct_hi, tril) + _bdot(gct_lo, tril)
                mask = upper
            qch = qc_s[rows, :]
            kch = kc_s[rows, :]
            vch = v_ref[rows, :].astype(jnp.bfloat16)
            cum_hi, cum_lo = _split_hi_lo(cum_col)
            for h in range(ML_HEADS):
                slab = slice((h // 2) * LANES, (h // 2 + 1) * LANES)
                own = even_lanes if h % 2 == 0 else jnp.logical_not(even_lanes)
                q_own = jnp.where(own, qch[:, slab], jnp.zeros((lc, LANES), jnp.bfloat16))
                idx = direction * ML_HEADS + h
                sel_i, sel_f = col_sel[direction * 8 + h], col_sel[direction * 8 + ML_HEADS + h]
                units.append(dict(
                    direction=direction, h=h, idx=idx, rows=rows, h_out=h_out, mask=mask, gct=gct,
                    cum_row=cum_row, kch=kch,
                    b_rep=_bdot(cum_hi, sel_f) + _bdot(cum_lo, sel_f),
                    i_rep=_bdot(gc_hi, sel_i) + _bdot(gc_lo, sel_i),
                    v_ext=jnp.concatenate([vch[:, h * ML_DV:(h + 1) * ML_DV], ones_ext], axis=1),
                    qk=lax.dot_general(q_own, kch[:, slab], (((1,), (1,)), ((), ())),
                                       preferred_element_type=jnp.float32),
                    qct=_bdot(q_own, ct_s[idx].astype(jnp.bfloat16))))

        for u in units:
            ci = u["direction"] * 8 + u["h"]
            cf = ci + ML_HEADS
            m = carry[u["idx"]]
            b_rep, i_rep = u["b_rep"], u["i_rep"]
            b_row = u["cum_row"][cf:cf + 1, :]
            i_row = u["gct"][ci:ci + 1, :]
            total = b_row[:, lc - 1:lc] if u["direction"] == 0 else b_row[:, 0:1]
            dmat = jnp.where(u["mask"], b_rep - b_row + i_row, NEG)
            row_max = jnp.broadcast_to(jnp.max(dmat, axis=1, keepdims=True), (lc, LANES))
            m_inter = m + b_rep
            m_t = jnp.maximum(m_inter, row_max)
            u["w_inter"] = jnp.exp(m_inter - m_t)
            u["floor"] = jnp.exp(-m_t)
            u["s"] = (u["qk"] * jnp.exp(dmat - m_t)).astype(jnp.bfloat16)
            g_row = total - b_row + i_row
            m_new = jnp.maximum(m + total, jnp.max(g_row, axis=1, keepdims=True))
            u["decay"] = jnp.exp(m + total - m_new)
            u["wk"] = jnp.exp(total - b_rep + i_rep - m_new)
            u["m_new"] = m_new

        for u in units:
            sv = _bdot(u["s"], u["v_ext"])
            num = u["w_inter"] * u["qct"][:, :ML_DV] + sv[:, :ML_DV]
            den = u["w_inter"] * u["qct"][:, ML_DV:] + sv[:, ML_DV:]
            u["h_out"][u["rows"], u["h"] * ML_DV:(u["h"] + 1) * ML_DV] = num / jnp.maximum(jnp.abs(den), u["floor"])

        for direction in (0, 1):
            us = units[direction * ML_HEADS:(direction + 1) * ML_HEADS]
            kf = us[0]["kch"].astype(jnp.float32)
            for pair in range(ML_HEADS // 2):
                wk_pair = jnp.where(even_lanes, us[2 * pair]["wk"], us[2 * pair + 1]["wk"])
                kwt = (kf[:, pair * LANES:(pair + 1) * LANES] * wk_pair).T.astype(jnp.bfloat16)
                for u in us[2 * pair:2 * pair + 2]:
                    own = even_rows if u["h"] % 2 == 0 else jnp.logical_not(even_rows)
                    kw_own = jnp.where(own, kwt, jnp.zeros((LANES, lc), jnp.bfloat16))
                    ct_s[u["idx"]] = u["decay"] * ct_s[u["idx"]] + _bdot(kw_own, u["v_ext"])
        return tuple(u["m_new"] for u in units)

    zero = jnp.zeros((1, 1), jnp.float32)
    lax.fori_loop(0, n_chunks, body, (zero,) * (2 * ML_HEADS))

    nw = nw_ref[...]

    def finish(c, carry):
        rows = pl.ds(pl.multiple_of(c * lc, lc), lc)
        hsum = hf_s[rows, :] + hb_s[rows, :]
        og = _sigmoid(o_ref[rows, :])
        for h in range(ML_HEADS):
            sl = slice(h * ML_DV, (h + 1) * ML_DV)
            hh = hsum[:, sl]
            mu = jnp.mean(hh, axis=-1, keepdims=True)
            hc = hh - mu
            var = jnp.mean(hc * hc, axis=-1, keepdims=True)
            y_ref[rows, sl] = (hc * lax.rsqrt(var + LN_EPS) * nw[:, sl] * og[:, sl]).astype(y_ref.dtype)
        return carry

    lax.fori_loop(0, n_chunks, finish, 0)


def _mlstm(z_ml, z_g, conv_w, gate_b, norm_w, layer, batch, seq):
    return pl.pallas_call(
        _mlstm_kernel,
        grid=(batch,),
        in_specs=[pl.BlockSpec((None, seq, ML_QK_W), lambda b: (b, 0, 0)),
                  pl.BlockSpec((None, seq, ML_QK_W), lambda b: (b, 0, 1)),
                  pl.BlockSpec((None, seq, ML_V_W), lambda b: (b, 0, 1)),
                  pl.BlockSpec((None, seq, ML_V_W), lambda b: (b, 0, 2)),
                  pl.BlockSpec((None, seq, LANES), lambda b: (b, 0, 0)),
                  pl.BlockSpec((None, 3, 2 * ML_QK_W), lambda b: (layer, 0, 0)),
                  pl.BlockSpec((None, 1, LANES), lambda b: (layer, 0, 0)),
                  pl.BlockSpec((None, 1, ML_V_W), lambda b: (layer, 0, 0))],
        out_specs=pl.BlockSpec((None, seq, ML_V_W), lambda b: (b, 0, 0)),
        out_shape=jax.ShapeDtypeStruct((batch, seq, ML_V_W), jnp.bfloat16),
        scratch_shapes=[pltpu.VMEM((seq, ML_QK_W), jnp.bfloat16),
                        pltpu.VMEM((seq, ML_QK_W), jnp.bfloat16),
                        pltpu.VMEM((seq, LANES), jnp.float32),
                        pltpu.VMEM((seq, ML_V_W), jnp.float32),
                        pltpu.VMEM((seq, ML_V_W), jnp.float32),
                        pltpu.VMEM((2 * ML_HEADS, 2 * ML_DQK, 2 * ML_DV), jnp.float32)],
        compiler_params=_cparams(("parallel",), 56),
    )(z_ml, z_ml, z_ml, z_ml, z_g, conv_w, gate_b, norm_w)


def _alibi_slopes():
    n = ATT_NGROUP * ATT_HEADS
    s = 2.0 ** (-8.0 * np.arange(1, n + 1) / n)
    return s.reshape(ATT_NGROUP, ATT_HEADS).astype(np.float32)


ATT_SLABS = ATT_GW // LANES
ATT_STAGE_ROWS = 256


def _attn_group(q_ref, k_ref, v_ref, y_ref, stage_s, qs, ks, vs, oacc, lacc, osub, lsub, mix_s, bias_s, *,
                group, seq, first, last, prev_dil):
    dil = ATT_DIL[group]
    assert ATT_WINDOW[group] // (2 * dil) == ATT_NEIGH and (not last or dil == 1)
    assert prev_dil is None or (prev_dil % dil == 0 and ATT_QT % (prev_dil // dil) == 0)
    sub_len = seq // dil
    slopes = [float(s) for s in _alibi_slopes()[group] * np.float32(dil)]
    kw = min(2 * ATT_QT, sub_len)
    tiles_per_sub = sub_len // ATT_QT

    if dil == 1:
        q_src, k_src, v_src = q_ref, k_ref, v_ref
    else:
        for src, dst in ((q_ref, qs), (k_ref, ks), (v_ref, vs)):
            def stage(c, carry, src=src):
                rows = pl.ds(pl.multiple_of(c * ATT_STAGE_ROWS, ATT_STAGE_ROWS), ATT_STAGE_ROWS)
                x = src[rows, :].astype(jnp.float32)
                for sl in range(ATT_SLABS):
                    stage_s[sl, rows, :] = x[:, sl * LANES:(sl + 1) * LANES]
                return carry

            lax.fori_loop(0, seq // ATT_STAGE_ROWS, stage, 0)

            def gather_residue(r, carry, dst=dst):
                out_rows = pl.ds(pl.multiple_of(r * sub_len, sub_len), sub_len)
                for sl in range(ATT_SLABS):
                    x = stage_s[sl, pl.ds(r, sub_len, stride=dil), :]
                    dst[out_rows, sl * LANES:(sl + 1) * LANES] = x.astype(jnp.bfloat16)
                return carry

            lax.fori_loop(0, dil, gather_residue, 0)
        q_src, k_src, v_src = qs, ks, vs

    low_half = lax.broadcasted_iota(jnp.int32, (1, LANES), 1) < ATT_DH

    n_offsets = 1 if tiles_per_sub == 1 else 3
    rel0 = (lax.broadcasted_iota(jnp.int32, (ATT_QT, kw), 1) - lax.broadcasted_iota(jnp.int32, (ATT_QT, kw), 0))
    for oi in range(n_offsets):
        arel = jnp.abs(rel0 - oi * ATT_NEIGH).astype(jnp.float32)
        for h in range(ATT_HEADS):
            bias_s[oi * ATT_HEADS + h, :, 0:kw] = jnp.where(arel <= float(ATT_NEIGH), -(slopes[h] * arel), NEG)

    def tile(idx, carry):
        r = idx // tiles_per_sub
        q0 = (idx % tiles_per_sub) * ATT_QT
        k0 = jnp.clip(q0 - ATT_NEIGH, 0, sub_len - kw)
        base = r * sub_len
        bias_base = (q0 - k0) // ATT_NEIGH * ATT_HEADS
        q = q_src[pl.ds(pl.multiple_of(base + q0, ATT_QT), ATT_QT), :] * (ATT_DH ** -0.5)
        k = k_src[pl.ds(pl.multiple_of(base + k0, ATT_NEIGH), kw), :]
        v = v_src[pl.ds(pl.multiple_of(base + k0, ATT_NEIGH), kw), :]
        start = q0 * dil + r
        nat_rows = pl.ds(start, ATT_QT) if dil == 1 else pl.ds(start, ATT_QT, stride=dil)
        scores = []
        for sl in range(ATT_SLABS):
            cols = slice(sl * LANES, (sl + 1) * LANES)
            q2, k2 = q[:, cols], k[:, cols]
            for half in range(2):
                own = low_half if half == 0 else jnp.logical_not(low_half)
                scores.append(lax.dot_general(jnp.where(own, q2, jnp.zeros_like(q2)), k2,
                                              (((1,), (1,)), ((), ())), preferred_element_type=jnp.float32))
        probs = []
        for h in range(ATT_HEADS):
            s = scores[h] + bias_s[bias_base + h, :, 0:kw]
            m = jnp.max(s, axis=1, keepdims=True)
            p = jnp.exp(s - m)
            den = jnp.sum(p, axis=1, keepdims=True)
            probs.append((p.astype(jnp.bfloat16), den, m + jnp.log(den)))
        for sl in range(ATT_SLABS):
            cols = slice(sl * LANES, (sl + 1) * LANES)
            v2 = v[:, cols]
            (p0, den0, lse0), (p1, den1, lse1) = probs[2 * sl], probs[2 * sl + 1]
            o_sl = jnp.where(low_half, _bdot(p0, v2) / den0, _bdot(p1, v2) / den1)
            l_sl = jnp.where(low_half, lse0, lse1)
            if first:
                sub_rows = pl.ds(pl.multiple_of(base + q0, ATT_QT), ATT_QT)
                osub[sl, sub_rows, :] = o_sl
                lsub[sl, sub_rows, :] = l_sl
                continue
            if prev_dil is not None:
                ratio = prev_dil // dil
                seg = ATT_QT // ratio
                for c in range(ratio):
                    src_rows = pl.ds(pl.multiple_of((dil * c + r) * (seq // prev_dil) + q0 // ratio, seg), seg)
                    mix_s[0, pl.ds(c, seg, stride=ratio), :] = osub[sl, src_rows, :]
                    mix_s[1, pl.ds(c, seg, stride=ratio), :] = lsub[sl, src_rows, :]
                o_old, l_old = mix_s[0], mix_s[1]
            else:
                o_old, l_old = oacc[sl, nat_rows, :], lacc[sl, nat_rows, :]
            mx = jnp.maximum(l_old, l_sl)
            w_old, w_new = jnp.exp(l_old - mx), jnp.exp(l_sl - mx)
            tot = w_old + w_new
            o_sl = (w_old * o_old + w_new * o_sl) / tot
            l_sl = mx + jnp.log(tot)
            if last:
                y_ref[pl.ds(pl.multiple_of(start, ATT_QT), ATT_QT), cols] = o_sl.astype(y_ref.dtype)
            else:
                oacc[sl, nat_rows, :] = o_sl
                lacc[sl, nat_rows, :] = l_sl
        return carry

    lax.fori_loop(0, seq // ATT_QT, tile, 0)


def _attn_kernel(q_ref, k_ref, v_ref, y_ref, *scratch, seq):
    gi = pl.program_id(1)
    for step in range(ATT_NGROUP):
        @pl.when(gi == step)
        def _(step=step):
            group = ATT_NGROUP - 1 - step
            _attn_group(q_ref, k_ref, v_ref, y_ref, *scratch, group=group, seq=seq,
                        first=step == 0, last=step == ATT_NGROUP - 1,
                        prev_dil=ATT_DIL[group + 1] if step == 1 else None)


def _dilated_attention(z_at, batch, seq):
    def in_spec(which):
        return pl.BlockSpec((None, seq, ATT_GW),
                            lambda b, gi: (b, 0, which * ATT_NGROUP + ATT_NGROUP - 1 - gi))

    slab = pltpu.VMEM((ATT_SLABS, seq, LANES), jnp.float32)
    sub = pltpu.VMEM((seq, ATT_GW), jnp.bfloat16)
    return pl.pallas_call(
        functools.partial(_attn_kernel, seq=seq),
        grid=(batch, ATT_NGROUP),
        in_specs=[in_spec(0), in_spec(1), in_spec(2)],
        out_specs=pl.BlockSpec((None, seq, ATT_GW), lambda b, gi: (b, 0, 0)),
        out_shape=jax.ShapeDtypeStruct((batch, seq, ATT_GW), jnp.bfloat16),
        scratch_shapes=[slab, sub, sub, sub, slab, slab, slab, slab,
                        pltpu.VMEM((2, ATT_QT, LANES), jnp.float32),
                        pltpu.VMEM((3 * ATT_HEADS, ATT_QT, 2 * ATT_QT), jnp.float32)],
        compiler_params=_cparams(("parallel", "arbitrary"), 56),
    )(z_at, z_at, z_at)


def _gelu_tanh(x):
    return x * (0.5 * (1.0 + jnp.tanh(np.sqrt(2.0 / np.pi).astype(np.float32) * (x + 0.044715 * (x * x * x)))))


def _sg_kernel(u_ref, v_ref, g_ref, b_ref, ws_ref, bs_ref, y_ref):
    tm = u_ref.shape[0]
    dg = BRANCH_W // SG_GROUPS
    u = _gelu_tanh(u_ref[...])
    vn = _layer_norm(_gelu_tanh(v_ref[...]), g_ref[...], b_ref[...]).astype(jnp.bfloat16)
    bs = bs_ref[...]
    for c in range(tm // SG_CHUNK):
        rows = slice(c * SG_CHUNK, (c + 1) * SG_CHUNK)
        for g in range(SG_GROUPS):
            cols = slice(g * dg, (g + 1) * dg)
            mixed = _bdot(ws_ref[g], vn[rows, cols]) + bs[:, cols]
            y_ref[rows, cols] = (u[rows, cols] * mixed).astype(y_ref.dtype)


def _spatial_gating(z_sp, ln_g, ln_b, w_s, b_s_exp, layer, tm):
    t = z_sp.shape[0]
    return pl.pallas_call(
        _sg_kernel,
        grid=(t // tm,),
        in_specs=[pl.BlockSpec((tm, BRANCH_W), lambda i: (i, 0)),
                  pl.BlockSpec((tm, BRANCH_W), lambda i: (i, 1)),
                  pl.BlockSpec((None, 1, BRANCH_W), lambda i: (layer, 0, 0)),
                  pl.BlockSpec((None, 1, BRANCH_W), lambda i: (layer, 0, 0)),
                  pl.BlockSpec((None, SG_GROUPS, SG_CHUNK, SG_CHUNK), lambda i: (layer, 0, 0, 0)),
                  pl.BlockSpec((None, SG_CHUNK, BRANCH_W), lambda i: (layer, 0, 0))],
        out_specs=pl.BlockSpec((tm, BRANCH_W), lambda i: (i, 0)),
        out_shape=jax.ShapeDtypeStruct((t, BRANCH_W), jnp.bfloat16),
        compiler_params=_cparams(("parallel",), 32),
    )(z_sp, z_sp, ln_g, ln_b, w_s, b_s_exp)


def _pool_kernel(p_ref, w_ref, sc_ref, y_ref):
    s_len = p_ref.shape[0]
    row = lax.broadcasted_iota(jnp.int32, (s_len, 1), 0)
    sc = sc_ref[...]
    for g, win in enumerate(POOL_WINDOWS):
        cols = slice(g * POOL_DG, (g + 1) * POOL_DG)
        half = win // 2
        p = p_ref[:, cols]
        acc = jnp.zeros_like(p)
        for j in range(-half, half):
            if j == 0:
                acc = acc + p
            else:
                shifted = pltpu.roll(p, (-j) % s_len, 0)
                ok = (row + j >= 0) & (row + j < s_len)
                acc = acc + jnp.where(ok, shifted, 0.0)
        cnt = (jnp.minimum(row + half, s_len) - jnp.maximum(row - half, 0)).astype(jnp.float32)
        d = acc / cnt - p
        y = _bdot(d.astype(jnp.bfloat16), w_ref[g]) * sc[:, cols]
        y_ref[:, cols] = y.astype(y_ref.dtype)


def _multiscale_pool(z_sp, pool_w, pool_scale, layer, batch, seq):
    return pl.pallas_call(
        _pool_kernel,
        grid=(batch,),
        in_specs=[pl.BlockSpec((None, seq, BRANCH_W), lambda b: (b, 0, 2)),
                  pl.BlockSpec((None, len(POOL_WINDOWS), POOL_DG, POOL_DG), lambda b: (layer, 0, 0, 0)),
                  pl.BlockSpec((None, 1, BRANCH_W), lambda b: (layer, 0, 0))],
        out_specs=pl.BlockSpec((None, seq, BRANCH_W), lambda b: (b, 0, 0)),
        out_shape=jax.ShapeDtypeStruct((batch, seq, BRANCH_W), jnp.bfloat16),
        compiler_params=_cparams(("parallel",), 48),
    )(z_sp, pool_w, pool_scale)


def _gate_mix_kernel(xb_ref, y0, y1, y2, y3, wg0, wg1, wg2, wg3, bg0, bg1, bg2, bg3, wb_ref, o_ref):
    xb = xb_ref[...]
    merged = None
    for n, (y, wg, bg) in enumerate(((y0, wg0, bg0), (y1, wg1, bg1), (y2, wg2, bg2), (y3, wg3, bg3))):
        gate = _sigmoid(_bdot(xb, wg[...]) + bg[...])
        term = gate * _bdot(y[...], wb_ref[n])
        merged = term if merged is None else merged + term
    o_ref[...] = merged.astype(o_ref.dtype)


def _gate_mix(xb, ys, w_gate, b_gate, w_branch, layer, tm, tn):
    t = xb.shape[0]
    nj = D_MODEL // tn
    row_spec = lambda w: pl.BlockSpec((tm, w), lambda i, j: (i, 0))
    wg_specs = [pl.BlockSpec((None, D_MODEL, tn), lambda i, j, n=n: (layer, 0, n * nj + j)) for n in range(N_BRANCH)]
    bg_specs = [pl.BlockSpec((None, 1, tn), lambda i, j, n=n: (layer, 0, n * nj + j)) for n in range(N_BRANCH)]
    return pl.pallas_call(
        _gate_mix_kernel,
        grid=(t // tm, nj),
        in_specs=[row_spec(D_MODEL)] + [row_spec(BRANCH_W)] * N_BRANCH + wg_specs + bg_specs
                 + [pl.BlockSpec((None, N_BRANCH, BRANCH_W, tn), lambda i, j: (layer, 0, 0, j))],
        out_specs=pl.BlockSpec((tm, tn), lambda i, j: (i, j)),
        out_shape=jax.ShapeDtypeStruct((t, D_MODEL), jnp.bfloat16),
        compiler_params=_cparams(("parallel", "arbitrary"), 48),
    )(xb, *ys, *([w_gate] * N_BRANCH), *([b_gate] * N_BRANCH), w_branch)


def _out_proj_ln_kernel(m_ref, wo_ref, xf_ref, g_ref, b_ref, wr_ref, br_ref, of_ref, ob_ref,
                        id_ref, wt_ref, cnt_ref):
    out = _layer_norm(ALPHA * xf_ref[...] + _bdot(m_ref[...], wo_ref[...]), g_ref[...], b_ref[...])
    of_ref[...] = out
    ob_ref[...] = out.astype(ob_ref.dtype)
    _route(out, wr_ref, br_ref, id_ref, wt_ref, cnt_ref)


def _out_proj_ln(merged, w_out, xf, ln_g, ln_b, w_r, b_r, layer, tm):
    t = xf.shape[0]
    row_spec = pl.BlockSpec((tm, D_MODEL), lambda i: (i, 0))
    vec_spec = pl.BlockSpec((None, 1, D_MODEL), lambda i: (layer, 0, 0))
    lane_spec = pl.BlockSpec((tm, LANES), lambda i: (i, 0))
    return pl.pallas_call(
        _out_proj_ln_kernel,
        grid=(t // tm,),
        in_specs=[row_spec, pl.BlockSpec((None, D_MODEL, D_MODEL), lambda i: (layer, 0, 0)),
                  row_spec, vec_spec, vec_spec,
                  pl.BlockSpec((None, 2, D_MODEL, LANES), lambda i: (layer, 0, 0, 0)),
                  pl.BlockSpec((None, 1, LANES), lambda i: (layer, 0, 0))],
        out_specs=[row_spec, row_spec, lane_spec, lane_spec, pl.BlockSpec((None, 8, LANES), lambda i: (i, 0, 0))],
        out_shape=[jax.ShapeDtypeStruct((t, D_MODEL), jnp.float32),
                   jax.ShapeDtypeStruct((t, D_MODEL), jnp.bfloat16),
                   jax.ShapeDtypeStruct((t, LANES), jnp.int32),
                   jax.ShapeDtypeStruct((t, LANES), jnp.float32),
                   jax.ShapeDtypeStruct((t // tm, 8, LANES), jnp.int32)],
        compiler_params=_cparams(("parallel",), 52),
    )(merged, w_out, xf, ln_g, ln_b, w_r, b_r)


def _route(x, w_ref, b_ref, id_ref, wt_ref, cnt_ref):
    x_hi, x_lo = _split_hi_lo(x)
    w_hi, w_lo = w_ref[0], w_ref[1]
    logits = _bdot(x_hi, w_hi) + _bdot(x_lo, w_hi) + _bdot(x_hi, w_lo) + b_ref[...]
    lane = lax.broadcasted_iota(jnp.int32, logits.shape, 1)
    lane_f = lane.astype(jnp.float32)
    no_lane = float(LANES)

    def first_argmax(vals):
        mx = jnp.max(vals, axis=1, keepdims=True)
        idx = jnp.min(jnp.where(vals == mx, lane_f, no_lane), axis=1, keepdims=True)
        return mx, idx.astype(jnp.int32)

    is_group = lane < N_EGROUPS
    g_max, g_sel = first_argmax(jnp.where(is_group, logits, NEG))
    p_sel = 1.0 / jnp.sum(jnp.where(is_group, jnp.exp(logits - g_max), 0.0), axis=1, keepdims=True)

    lo = ROUTER_EOFF + g_sel * EXP_PER_GROUP
    cand = jnp.where((lane >= lo) & (lane < lo + EXP_PER_GROUP), logits, NEG)
    l1, i1 = first_argmax(cand)
    l2, i2 = first_argmax(jnp.where(lane == i1, NEG, cand))
    e = jnp.exp(l2 - l1)
    w1 = p_sel / (1.0 + e)
    w2 = p_sel * e / (1.0 + e)
    e1, e2 = i1 - ROUTER_EOFF, i2 - ROUTER_EOFF

    tm = logits.shape[0]
    hit1, hit2 = lane == e1, lane == e2
    hits = jnp.where(jnp.logical_or(hit1, hit2), 1.0, 0.0)
    earlier = (lax.broadcasted_iota(jnp.int32, (tm, tm), 1) < lax.broadcasted_iota(jnp.int32, (tm, tm), 0))
    before = _bdot(jnp.where(earlier, 1.0, 0.0).astype(jnp.bfloat16), hits.astype(jnp.bfloat16))
    rank1 = jnp.sum(jnp.where(hit1, before, 0.0), axis=1, keepdims=True).astype(jnp.int32)
    rank2 = jnp.sum(jnp.where(hit2, before, 0.0), axis=1, keepdims=True).astype(jnp.int32)
    counts = jnp.sum(hits, axis=0, keepdims=True).astype(jnp.int32)

    id_ref[...] = jnp.where(lane == 0, e1, jnp.where(lane == 1, e2,
                            jnp.where(lane == 2, rank1, jnp.where(lane == 3, rank2, 0))))
    wt_ref[...] = jnp.where(lane == 0, w1, jnp.where(lane == 1, w2, 0.0))
    cnt_ref[...] = jnp.broadcast_to(counts, cnt_ref.shape)


def _dispatch_plan(info, counts, blk, tm):
    t = info.shape[0]
    c = counts[:, 0, :N_EXPERTS]
    tile_off = jnp.cumsum(c, axis=0) - c
    total = jnp.sum(c, axis=0)
    padded = (total + blk - 1) // blk * blk
    pad_end = jnp.cumsum(padded)
    base = (pad_end - padded)[None, :] + tile_off
    base_tok = jnp.repeat(base, tm, axis=0)
    hit = info[:, :TOP_K, None] == jnp.arange(N_EXPERTS, dtype=jnp.int32)[None, None, :]
    dest = jnp.sum(jnp.where(hit, base_tok[:, None, :], 0), axis=2) + info[:, TOP_K:2 * TOP_K]
    nb = t * TOP_K // blk + N_EXPERTS
    blk_start = jnp.arange(nb, dtype=jnp.int32) * blk
    blk_e = jnp.minimum(jnp.sum((pad_end[None, :] <= blk_start[:, None]).astype(jnp.int32), axis=1),
                        N_EXPERTS - 1)
    n_used = (pad_end[-1] // blk).astype(jnp.int32).reshape(1)
    return dest.astype(jnp.int32), blk_e, n_used


def _dispatch_kernel(dest_ref, dest_prev_ref, x_ref, xs_in, w1_ref, w3_ref, w2_ref,
                     xs_out, w1_out, w3_out, w2_out, pk, sem):
    del xs_in
    w1_out[...] = w1_ref[...].astype(w1_out.dtype)
    w3_out[...] = w3_ref[...].astype(w3_out.dtype)
    w2_out[...] = w2_ref[...].astype(w2_out.dtype)
    i = pl.program_id(0)
    tm = x_ref.shape[0]
    half = D_MODEL // 2
    slot = lax.rem(i, 2)
    grp = ROW_DMA_GROUP

    def copy(idx_ref, s, r, j):
        return pltpu.make_async_copy(pk.at[s, pl.ds(r, 1), :], xs_out.at[pl.ds(idx_ref[0, j], 1), :], sem.at[s])

    def pack(t):
        rows = pl.ds(pl.multiple_of(t * grp, grp), grp)
        x = x_ref[rows, :].astype(jnp.float32)
        hi = lax.bitcast_convert_type(x[:, :half], jnp.uint32)
        lo = lax.bitcast_convert_type(x[:, half:], jnp.uint32)
        pk[slot, rows, :] = hi | (lo >> 16)

    def issue(t):
        for u in range(grp):
            for k in range(TOP_K):
                copy(dest_ref, slot, t * grp + u, k * tm + t * grp + u).start(priority=k)

    def group(t, carry):
        pack(t + 1)
        issue(t)
        return carry

    n_groups = tm // grp
    pack(0)
    lax.fori_loop(0, n_groups - 1, group, 0)
    issue(n_groups - 1)

    def wait_prev(j, carry):
        copy(dest_prev_ref, 1 - slot, lax.rem(j, tm), j).wait()
        return carry

    @pl.when(i > 0)
    def _():
        lax.fori_loop(0, TOP_K * tm, wait_prev, 0, unroll=8)

    def wait_own(j, carry):
        copy(dest_ref, slot, lax.rem(j, tm), j).wait()
        return carry

    @pl.when(i == pl.num_programs(0) - 1)
    def _():
        lax.fori_loop(0, TOP_K * tm, wait_own, 0, unroll=8)


def _dest_tiles(dest, tm):
    nt = dest.shape[0] // tm
    return dest.reshape(nt, tm, TOP_K).transpose(0, 2, 1).reshape(nt, 1, TOP_K * tm)


def _dispatch_rows(xb, dest_tiles, xs_prev, w1, w3, w2, layer):
    nt, _, two_tm = dest_tiles.shape
    tm = two_tm // TOP_K
    split = max(nt // N_EXPERTS, 1)
    group = max(N_EXPERTS // nt, 1)
    assert split * N_EXPERTS == nt * group

    def w_spec(rows, cols):
        return pl.BlockSpec((None, group, rows // split, cols), lambda i: (layer, i // split, i % split, 0))

    def w_out_spec(rows, cols):
        return pl.BlockSpec((group, rows // split, cols), lambda i: (i // split, i % split, 0))

    return pl.pallas_call(
        _dispatch_kernel,
        grid=(nt,),
        in_specs=[pl.BlockSpec((None, 1, two_tm), lambda i: (i, 0, 0), memory_space=pltpu.SMEM),
                  pl.BlockSpec((None, 1, two_tm), lambda i: (jnp.maximum(i - 1, 0), 0, 0),
                               memory_space=pltpu.SMEM),
                  pl.BlockSpec((tm, D_MODEL), lambda i: (i, 0)),
                  pl.BlockSpec(memory_space=pl.ANY),
                  w_spec(D_MODEL, D_EXPERT), w_spec(D_MODEL, D_EXPERT), w_spec(D_EXPERT, D_MODEL)],
        out_specs=[pl.BlockSpec(memory_space=pl.ANY),
                   w_out_spec(D_MODEL, D_EXPERT), w_out_spec(D_MODEL, D_EXPERT), w_out_spec(D_EXPERT, D_MODEL)],
        out_shape=[jax.ShapeDtypeStruct(xs_prev.shape, xs_prev.dtype),
                   jax.ShapeDtypeStruct((N_EXPERTS, D_MODEL, D_EXPERT), jnp.bfloat16),
                   jax.ShapeDtypeStruct((N_EXPERTS, D_MODEL, D_EXPERT), jnp.bfloat16),
                   jax.ShapeDtypeStruct((N_EXPERTS, D_EXPERT, D_MODEL), jnp.bfloat16)],
        scratch_shapes=[pltpu.VMEM((2, tm, D_MODEL // 2), jnp.uint32),
                        pltpu.SemaphoreType.DMA((2,))],
        input_output_aliases={3: 0},
        compiler_params=_cparams(("arbitrary",), 48),
    )(dest_tiles, dest_tiles, xb, xs_prev, w1, w3, w2)


def _row_copy(src_hbm, idx, buf, slot, r, sem):
    return pltpu.make_async_copy(src_hbm.at[pl.ds(idx, 1), :], buf.at[slot, pl.ds(r, 1), :], sem.at[slot])


def _start_row_gather(src_hbm, idx_ref, buf, slot, sem, n_rows):
    def body(g, carry):
        for u in range(ROW_DMA_UNROLL):
            r = g * ROW_DMA_UNROLL + u
            _row_copy(src_hbm, idx_ref[0, r], buf, slot, r, sem).start(priority=u % 2)
        return carry
    lax.fori_loop(0, n_rows // ROW_DMA_UNROLL, body, 0)


def _wait_row_gather(src_hbm, idx_ref, buf, slot, sem, n_rows):
    def body(r, carry):
        _row_copy(src_hbm, idx_ref[0, r], buf, slot, r, sem).wait()
        return carry
    lax.fori_loop(0, n_rows, body, 0, unroll=8)


def _expert_kernel(blk_e_ref, n_used_ref, xs_ref, w1_ref, w3_ref, w2_ref, y_ref):
    del blk_e_ref
    i = pl.program_id(0)
    n_used = n_used_ref[0]

    @pl.when(i < n_used)
    def _():
        packed = xs_ref[...]
        x_a = lax.bitcast_convert_type(packed & jnp.uint32(0xFFFF0000), jnp.float32)
        x_b = lax.bitcast_convert_type(packed << 16, jnp.float32)
        xb = jnp.concatenate([x_a, x_b], axis=1).astype(jnp.bfloat16)
        a = _bdot(xb, w1_ref[...])
        h = (a * _sigmoid(a)) * _bdot(xb, w3_ref[...])
        y_ref[...] = _bdot(h.astype(jnp.bfloat16), w2_ref[...])

    @pl.when(i >= n_used)
    def _():
        y_ref[...] = jnp.zeros(y_ref.shape, y_ref.dtype)


def _experts(xs, blk_e, n_used, w1, w3, w2, blk):
    nb = xs.shape[0] // blk

    def used(i, nu):
        return jnp.minimum(i, nu[0] - 1)

    def w_index(i, be, nu):
        return (be[used(i, nu)], 0, 0)

    grid_spec = pltpu.PrefetchScalarGridSpec(
        num_scalar_prefetch=2,
        grid=(nb,),
        in_specs=[pl.BlockSpec((blk, D_MODEL // 2), lambda i, be, nu: (used(i, nu), 0)),
                  pl.BlockSpec((None, D_MODEL, D_EXPERT), w_index),
                  pl.BlockSpec((None, D_MODEL, D_EXPERT), w_index),
                  pl.BlockSpec((None, D_EXPERT, D_MODEL), w_index)],
        out_specs=pl.BlockSpec((blk, D_MODEL), lambda i, be, nu: (i, 0)))
    return pl.pallas_call(
        _expert_kernel,
        grid_spec=grid_spec,
        out_shape=jax.ShapeDtypeStruct((nb * blk, D_MODEL), jnp.float32),
        compiler_params=_cparams(("arbitrary",), 40),
    )(blk_e, n_used, xs, w1, w3, w2)


def _combine_kernel(pos_ref, pos_next_ref, y_hbm, xf_ref, wt_ref, g_ref, b_ref, of_ref, ob_ref, ybuf, sem):
    i = pl.program_id(0)
    tm = xf_ref.shape[0]
    slot = lax.rem(i, 2)

    @pl.when(i == 0)
    def _():
        _start_row_gather(y_hbm, pos_ref, ybuf, 0, sem, TOP_K * tm)

    _wait_row_gather(y_hbm, pos_ref, ybuf, slot, sem, TOP_K * tm)
    gain, shift = g_ref[...], b_ref[...]
    grp = ROW_DMA_GROUP

    def make_group(issue_next):
        def group(t, carry):
            r0 = pl.multiple_of(t * grp, grp)
            rows = pl.ds(r0, grp)
            wt = wt_ref[rows, :]
            moe = wt[:, 0:1] * ybuf[slot, rows, :] + wt[:, 1:2] * ybuf[slot, pl.ds(tm + r0, grp), :]
            out = _layer_norm(ALPHA * xf_ref[rows, :] + moe, gain, shift)
            of_ref[rows, :] = out
            ob_ref[rows, :] = out.astype(ob_ref.dtype)
            if issue_next:
                for u in range(TOP_K * grp):
                    j = t * (TOP_K * grp) + u
                    _row_copy(y_hbm, pos_next_ref[0, j], ybuf, 1 - slot, j, sem).start(priority=u % 2)
            return carry
        return group

    has_next = i + 1 < pl.num_programs(0)

    @pl.when(has_next)
    def _():
        lax.fori_loop(0, tm // grp, make_group(True), 0)

    @pl.when(jnp.logical_not(has_next))
    def _():
        lax.fori_loop(0, tm // grp, make_group(False), 0)


def _combine(y_sorted, pos_tiles, xf, wts, ln_g, ln_b, layer):
    t = xf.shape[0]
    nt = pos_tiles.shape[0]
    tm = t // nt
    row_spec = pl.BlockSpec((tm, D_MODEL), lambda i: (i, 0))
    vec_spec = pl.BlockSpec((None, 1, D_MODEL), lambda i: (layer, 0, 0))
    return pl.pallas_call(
        _combine_kernel,
        grid=(nt,),
        in_specs=[pl.BlockSpec((None, 1, TOP_K * tm), lambda i: (i, 0, 0), memory_space=pltpu.SMEM),
                  pl.BlockSpec((None, 1, TOP_K * tm), lambda i: (jnp.minimum(i + 1, nt - 1), 0, 0),
                               memory_space=pltpu.SMEM),
                  pl.BlockSpec(memory_space=pl.ANY),
                  row_spec,
                  pl.BlockSpec((tm, LANES), lambda i: (i, 0)),
                  vec_spec, vec_spec],
        out_specs=[row_spec, row_spec],
        out_shape=[jax.ShapeDtypeStruct((t, D_MODEL), jnp.float32),
                   jax.ShapeDtypeStruct((t, D_MODEL), jnp.bfloat16)],
        scratch_shapes=[pltpu.VMEM((2, TOP_K * tm, D_MODEL), jnp.float32),
                        pltpu.SemaphoreType.DMA((2,))],
        compiler_params=_cparams(("arbitrary",), 48),
    )(pos_tiles, pos_tiles, y_sorted, xf, wts, ln_g, ln_b)


MOE_BLOCK_ROWS = 256
PROJ_TM, PROJ_TN = 2048, 512
MIX_TM, MIX_TN = 1024, 256
OUT_TM = 512
SG_TM = 512
ROW_DMA_TM = 256


def _mixer_layer(xf, xb, p, layer, batch, seq):
    t = batch * seq
    z_ml = _matmul(xb, p["w_ml"], layer, jnp.float32, PROJ_TM, PROJ_TN)
    z_g = _matmul(xb, p["w_mg"], layer, jnp.float32, PROJ_TM, LANES)
    z_at = _matmul(xb, p["w_at"], layer, jnp.bfloat16, PROJ_TM, PROJ_TN)
    z_sp = _matmul(xb, p["w_sp"], layer, jnp.float32, PROJ_TM, PROJ_TN)
    y_ml = _mlstm(z_ml.reshape(batch, seq, ML_W), z_g.reshape(batch, seq, LANES),
                  p["ml_conv_w"], p["ml_gate_b"], p["ml_norm_w"], layer, batch, seq).reshape(t, BRANCH_W)
    y_at = _dilated_attention(z_at.reshape(batch, seq, 3 * ATT_W), batch, seq).reshape(t, BRANCH_W)
    y_sg = _spatial_gating(z_sp, p["sg_ln_g"], p["sg_ln_b"], p["sg_w"], p["sg_b"], layer, SG_TM)
    y_pl = _multiscale_pool(z_sp.reshape(batch, seq, 3 * BRANCH_W), p["pool_w"], p["pool_scale"],
                            layer, batch, seq).reshape(t, BRANCH_W)
    merged = _gate_mix(xb, (y_ml, y_at, y_sg, y_pl), p["w_gate"], p["b_gate"], p["w_branch"], layer,
                       MIX_TM, MIX_TN)
    return _out_proj_ln(merged, p["w_out"], xf, p["ln1_g"], p["ln1_b"], p["w_router"], p["b_router"],
                        layer, OUT_TM)


def _moe_layer(xf, xb, routing, xs_prev, p, layer):
    info, wts, counts = routing
    dest, blk_e, n_used = _dispatch_plan(info, counts, MOE_BLOCK_ROWS, OUT_TM)
    dest_tiles = _dest_tiles(dest, ROW_DMA_TM)
    xs, w1, w3, w2 = _dispatch_rows(xb, dest_tiles, xs_prev, p["w_exp_gate"], p["w_exp_up"], p["w_exp_down"],
                                    layer)
    y_sorted = _experts(xs, blk_e, n_used, w1, w3, w2, MOE_BLOCK_ROWS)
    xf, xb = _combine(y_sorted, dest_tiles, xf, wts, p["ln2_g"], p["ln2_b"], layer)
    return xf, xb, xs


def _prepare_params(w_in, ml_conv_w, ml_gate_b, ml_norm_w, sg_ln_g, sg_ln_b, sg_w, sg_b, pool_w, pool_scale,
                    w_gate, b_gate, w_branch, w_out, ln1_g, ln1_b, w_router_group, b_router_group,
                    w_router_expert, b_router_expert, w_exp_gate, w_exp_up, w_exp_down, ln2_g, ln2_b):
    bf = jnp.bfloat16
    depth = w_in.shape[0]
    c_ml, c_g = ML_W, ML_W + ML_GATES
    c_at = c_g + 3 * ATT_W
    row = lambda a: a.reshape(depth, 1, -1)
    lane_pad = lambda a: jnp.pad(a, [(0, 0)] * (a.ndim - 1) + [(0, LANES - a.shape[-1])])
    w_r = jnp.zeros((depth, D_MODEL, LANES), jnp.float32)
    w_r = w_r.at[:, :, :N_EGROUPS].set(w_router_group).at[:, :, ROUTER_EOFF:ROUTER_EOFF + N_EXPERTS].set(w_router_expert)
    b_r = jnp.zeros((depth, LANES), jnp.float32)
    b_r = b_r.at[:, :N_EGROUPS].set(b_router_group).at[:, ROUTER_EOFF:ROUTER_EOFF + N_EXPERTS].set(b_router_expert)
    w_r_hi = w_r.astype(bf)
    w_r_lo = (w_r - w_r_hi.astype(jnp.float32)).astype(bf)
    sg_b_exp = jnp.repeat(jnp.swapaxes(sg_b, 1, 2), BRANCH_W // SG_GROUPS, axis=2)
    return dict(
        w_ml=w_in[:, :, :c_ml].astype(bf),
        w_mg=lane_pad(w_in[:, :, c_ml:c_g]).astype(bf),
        w_at=w_in[:, :, c_g:c_at].astype(bf),
        w_sp=w_in[:, :, c_at:].astype(bf),
        ml_conv_w=ml_conv_w,
        ml_gate_b=lane_pad(ml_gate_b.reshape(depth, 1, ML_GATES)),
        ml_norm_w=row(ml_norm_w),
        sg_ln_g=row(sg_ln_g), sg_ln_b=row(sg_ln_b), sg_w=sg_w.astype(bf), sg_b=sg_b_exp,
        pool_w=pool_w.astype(bf), pool_scale=row(pool_scale),
        w_gate=w_gate.astype(bf), b_gate=row(b_gate), w_branch=w_branch.astype(bf), w_out=w_out.astype(bf),
        ln1_g=row(ln1_g), ln1_b=row(ln1_b),
        w_router=jnp.stack([w_r_hi, w_r_lo], axis=1), b_router=row(b_r),
        w_exp_gate=w_exp_gate, w_exp_up=w_exp_up, w_exp_down=w_exp_down,
        ln2_g=row(ln2_g), ln2_b=row(ln2_b))


def kernel(x, w_in, ml_conv_w, ml_gate_b, ml_norm_w, sg_ln_g, sg_ln_b, sg_w, sg_b, pool_w, pool_scale, w_gate,
           b_gate, w_branch, w_out, ln1_g, ln1_b, w_router_group, b_router_group, w_router_expert,
           b_router_expert, w_exp_gate, w_exp_up, w_exp_down, ln2_g, ln2_b):
    batch, seq, d = x.shape
    assert d == D_MODEL and seq % (ATT_DIL[-1] * ATT_QT) == 0
    p = _prepare_params(w_in, ml_conv_w, ml_gate_b, ml_norm_w, sg_ln_g, sg_ln_b, sg_w, sg_b, pool_w, pool_scale,
                        w_gate, b_gate, w_branch, w_out, ln1_g, ln1_b, w_router_group, b_router_group,
                        w_router_expert, b_router_expert, w_exp_gate, w_exp_up, w_exp_down, ln2_g, ln2_b)
    t = batch * seq
    xf = x.reshape(t, d)
    xb = xf.astype(jnp.bfloat16)
    n_slots = (t * TOP_K // MOE_BLOCK_ROWS + N_EXPERTS) * MOE_BLOCK_ROWS
    xs = jnp.zeros((n_slots, d // 2), jnp.uint32)
    for layer in range(w_in.shape[0]):
        xf, xb, *routing = _mixer_layer(xf, xb, p, layer, batch, seq)
        xf, xb, xs = _moe_layer(xf, xb, routing, xs, p, layer)
    return xf.reshape(batch, seq, d)
```

```python
import functools

import numpy as np
import jax
import jax.numpy as jnp
from jax import lax
from jax.experimental import pallas as pl
from jax.experimental.pallas import tpu as pltpu

D_MODEL = 2048
DEPTH = 4
BRANCH_W = 512
N_BRANCH = 4

ML_HEADS = 4
ML_DQK = 64
ML_DV = 128
ML_CHUNK = 128
ML_QK_W = ML_HEADS * ML_DQK
ML_V_W = ML_HEADS * ML_DV
ML_GATES = 2 * 2 * ML_HEADS
ML_W = 2 * ML_QK_W + 2 * ML_V_W

ATT_WINDOW = (128, 512, 2048)
ATT_DIL = (1, 4, 16)
ATT_NGROUP = 3
ATT_HEADS = 8
ATT_DH = 64
ATT_GW = ATT_HEADS * ATT_DH
ATT_W = ATT_NGROUP * ATT_GW
ATT_NEIGH = 64
ATT_QT = 128

SG_CHUNK = 128
SG_GROUPS = 4
POOL_WINDOWS = (2, 4, 8, 16)
POOL_DG = BRANCH_W // len(POOL_WINDOWS)

N_EGROUPS = 4
EXP_PER_GROUP = 8
N_EXPERTS = N_EGROUPS * EXP_PER_GROUP
TOP_K = 2
D_EXPERT = 512
ROUTER_EOFF = 32

ALPHA = (2.0 * DEPTH) ** 0.25
LN_EPS = 1e-5
NEG = -1e30

ROW_DMA_UNROLL = 8
ROW_DMA_GROUP = 64

LANES = 128
VMEM_BYTES_V7X = 64 * 1024 * 1024


def _cparams(semantics, vmem_mb):
    assert vmem_mb * 1024 * 1024 < VMEM_BYTES_V7X
    return pltpu.CompilerParams(dimension_semantics=semantics, vmem_limit_bytes=vmem_mb * 1024 * 1024)


def _sigmoid(x):
    return 1.0 / (1.0 + jnp.exp(-x))


def _bdot(a, b):
    return jnp.dot(a, b, preferred_element_type=jnp.float32)


def _layer_norm(x, g, b):
    mu = jnp.mean(x, axis=-1, keepdims=True)
    xc = x - mu
    var = jnp.mean(xc * xc, axis=-1, keepdims=True)
    return xc * lax.rsqrt(var + LN_EPS) * g + b


def _mm_kernel(x_ref, w_ref, o_ref):
    o_ref[...] = _bdot(x_ref[...], w_ref[...]).astype(o_ref.dtype)


def _matmul(x, w, layer, out_dtype, tm, tn):
    t, k = x.shape
    n = w.shape[2]
    return pl.pallas_call(
        _mm_kernel,
        grid=(t // tm, n // tn),
        in_specs=[pl.BlockSpec((tm, k), lambda i, j: (i, 0)),
                  pl.BlockSpec((None, k, tn), lambda i, j: (layer, 0, j))],
        out_specs=pl.BlockSpec((tm, tn), lambda i, j: (i, j)),
        out_shape=jax.ShapeDtypeStruct((t, n), out_dtype),
        compiler_params=_cparams(("parallel", "arbitrary"), 40),
    )(x, w)


def _split_hi_lo(x):
    hi = x.astype(jnp.bfloat16)
    lo = (x - hi.astype(jnp.float32)).astype(jnp.bfloat16)
    return hi, lo


def _mlstm_kernel(q_ref, k_ref, v_ref, o_ref, g_ref, cw_ref, gb_ref, nw_ref, y_ref,
                  qc_s, kc_s, gp_s, hf_s, hb_s, ct_s):
    s_len = q_ref.shape[0]
    n_chunks = s_len // ML_CHUNK
    lc = ML_CHUNK

    row_c = lax.broadcasted_iota(jnp.int32, (lc, 1), 0)
    cw = cw_ref[...]
    gate_lane = lax.broadcasted_iota(jnp.int32, (lc, LANES), 1)

    def prep(c, carry):
        r0 = pl.multiple_of(c * lc, lc)
        rows = pl.ds(r0, lc)

        def conv_silu(ref, w):
            x = ref[rows, :]
            prev_row = jnp.where(c > 0, ref[pl.ds(jnp.maximum(r0 - 1, 0), 1), :], 0.0)
            next_row = jnp.where(c < n_chunks - 1, ref[pl.ds(jnp.minimum(r0 + lc, s_len - 1), 1), :], 0.0)
            x_prev = jnp.where(row_c == 0, prev_row, pltpu.roll(x, 1, 0))
            x_next = jnp.where(row_c == lc - 1, next_row, pltpu.roll(x, lc - 1, 0))
            y = x_prev * w[0:1] + x * w[1:2] + x_next * w[2:3]
            return y * _sigmoid(y)

        qc_s[rows, :] = (conv_silu(q_ref, cw[:, :ML_QK_W]) * (ML_DQK ** -0.5)).astype(jnp.bfloat16)
        kc_s[rows, :] = conv_silu(k_ref, cw[:, ML_QK_W:]).astype(jnp.bfloat16)

        gp = g_ref[rows, :] + gb_ref[...]
        log_sig = jnp.minimum(gp, 0.0) - jnp.log(1.0 + jnp.exp(-jnp.abs(gp)))
        gp_s[rows, :] = jnp.where((gate_lane % 8) >= ML_HEADS, log_sig, gp)
        return carry

    lax.fori_loop(0, n_chunks, prep, 0)

    r_i = lax.broadcasted_iota(jnp.int32, (lc, lc), 0)
    c_i = lax.broadcasted_iota(jnp.int32, (lc, lc), 1)
    lower = c_i <= r_i
    upper = c_i >= r_i
    tril = jnp.where(lower, 1.0, 0.0).astype(jnp.bfloat16)
    triu = jnp.where(upper, 1.0, 0.0).astype(jnp.bfloat16)
    ones_ext = jnp.ones((lc, ML_DV), jnp.bfloat16)
    sel_r = lax.broadcasted_iota(jnp.int32, (LANES, LANES), 0)
    col_sel = [jnp.where(sel_r == j, 1.0, 0.0).astype(jnp.bfloat16) for j in range(ML_GATES)]

    even_lanes = lax.broadcasted_iota(jnp.int32, (1, LANES), 1) < ML_DQK
    even_rows = lax.broadcasted_iota(jnp.int32, (LANES, 1), 0) < ML_DQK

    ct_s[...] = jnp.zeros(ct_s.shape, ct_s.dtype)

    def body(c, carry):
        units = []
        for direction, h_out in ((0, hf_s), (1, hb_s)):
            cc = c if direction == 0 else n_chunks - 1 - c
            rows = pl.ds(pl.multiple_of(cc * lc, lc), lc)
            gc = gp_s[rows, :]
            gct = gc.T
            gc_hi, gc_lo = _split_hi_lo(gc)
            gct_hi, gct_lo = _split_hi_lo(gct)
            if direction == 0:
                cum_col = _bdot(tril, gc_hi) + _bdot(tril, gc_lo)
                cum_row = _bdot(gct_hi, triu) + _bdot(gct_lo, triu)
                mask = lower
            else:
                cum_col = _bdot(triu, gc_hi) + _bdot(triu, gc_lo)
                cum_row = _bdot(gct_hi, tril) + _bdot(gct_lo, tril)
                mask = upper
            qch = qc_s[rows, :]
            kch = kc_s[rows, :]
            vch = v_ref[rows, :].astype(jnp.bfloat16)
            cum_hi, cum_lo = _split_hi_lo(cum_col)
            for h in range(ML_HEADS):
                slab = slice((h // 2) * LANES, (h // 2 + 1) * LANES)
                own = even_lanes if h % 2 == 0 else jnp.logical_not(even_lanes)
                q_own = jnp.where(own, qch[:, slab], jnp.zeros((lc, LANES), jnp.bfloat16))
                idx = direction * ML_HEADS + h
                sel_i, sel_f = col_sel[direction * 8 + h], col_sel[direction * 8 + ML_HEADS + h]
                units.append(dict(
                    direction=direction, h=h, idx=idx, rows=rows, h_out=h_out, mask=mask, gct=gct,
                    cum_row=cum_row, kch=kch,
                    b_rep=_bdot(cum_hi, sel_f) + _bdot(cum_lo, sel_f),
                    i_rep=_bdot(gc_hi, sel_i) + _bdot(gc_lo, sel_i),
                    v_ext=jnp.concatenate([vch[:, h * ML_DV:(h + 1) * ML_DV], ones_ext], axis=1),
                    qk=lax.dot_general(q_own, kch[:, slab], (((1,), (1,)), ((), ())),
                                       preferred_element_type=jnp.float32),
                    qct=_bdot(q_own, ct_s[idx].astype(jnp.bfloat16))))

        for u in units:
            ci = u["direction"] * 8 + u["h"]
            cf = ci + ML_HEADS
            m = carry[u["idx"]]
            b_rep, i_rep = u["b_rep"], u["i_rep"]
            b_row = u["cum_row"][cf:cf + 1, :]
            i_row = u["gct"][ci:ci + 1, :]
            total = b_row[:, lc - 1:lc] if u["direction"] == 0 else b_row[:, 0:1]
            dmat = jnp.where(u["mask"], b_rep - b_row + i_row, NEG)
            row_max = jnp.broadcast_to(jnp.max(dmat, axis=1, keepdims=True), (lc, LANES))
            m_inter = m + b_rep
            m_t = jnp.maximum(m_inter, row_max)
            u["w_inter"] = jnp.exp(m_inter - m_t)
            u["floor"] = jnp.exp(-m_t)
            u["s"] = (u["qk"] * jnp.exp(dmat - m_t)).astype(jnp.bfloat16)
            g_row = total - b_row + i_row
            m_new = jnp.maximum(m + total, jnp.max(g_row, axis=1, keepdims=True))
            u["decay"] = jnp.exp(m + total - m_new)
            u["wk"] = jnp.exp(total - b_rep + i_rep - m_new)
            u["m_new"] = m_new

        for u in units:
            sv = _bdot(u["s"], u["v_ext"])
            num = u["w_inter"] * u["qct"][:, :ML_DV] + sv[:, :ML_DV]
            den = u["w_inter"] * u["qct"][:, ML_DV:] + sv[:, ML_DV:]
            u["h_out"][u["rows"], u["h"] * ML_DV:(u["h"] + 1) * ML_DV] = num / jnp.maximum(jnp.abs(den), u["floor"])

        for direction in (0, 1):
            us = units[direction * ML_HEADS:(direction + 1) * ML_HEADS]
            kf = us[0]["kch"].astype(jnp.float32)
            for pair in range(ML_HEADS // 2):
                wk_pair = jnp.where(even_lanes, us[2 * pair]["wk"], us[2 * pair + 1]["wk"])
                kwt = (kf[:, pair * LANES:(pair + 1) * LANES] * wk_pair).T.astype(jnp.bfloat16)
                for u in us[2 * pair:2 * pair + 2]:
                    own = even_rows if u["h"] % 2 == 0 else jnp.logical_not(even_rows)
                    kw_own = jnp.where(own, kwt, jnp.zeros((LANES, lc), jnp.bfloat16))
                    ct_s[u["idx"]] = u["decay"] * ct_s[u["idx"]] + _bdot(kw_own, u["v_ext"])
        return tuple(u["m_new"] for u in units)

    zero = jnp.zeros((1, 1), jnp.float32)
    lax.fori_loop(0, n_chunks, body, (zero,) * (2 * ML_HEADS))

    nw = nw_ref[...]

    def finish(c, carry):
        rows = pl.ds(pl.multiple_of(c * lc, lc), lc)
        hsum = hf_s[rows, :] + hb_s[rows, :]
        og = _sigmoid(o_ref[rows, :])
        for h in range(ML_HEADS):
            sl = slice(h * ML_DV, (h + 1) * ML_DV)
            hh = hsum[:, sl]
            mu = jnp.mean(hh, axis=-1, keepdims=True)
            hc = hh - mu
            var = jnp.mean(hc * hc, axis=-1, keepdims=True)
            y_ref[rows, sl] = (hc * lax.rsqrt(var + LN_EPS) * nw[:, sl] * og[:, sl]).astype(y_ref.dtype)
        return carry

    lax.fori_loop(0, n_chunks, finish, 0)


def _mlstm(z_ml, z_g, conv_w, gate_b, norm_w, layer, batch, seq):
    return pl.pallas_call(
        _mlstm_kernel,
        grid=(batch,),
        in_specs=[pl.BlockSpec((None, seq, ML_QK_W), lambda b: (b, 0, 0)),
                  pl.BlockSpec((None, seq, ML_QK_W), lambda b: (b, 0, 1)),
                  pl.BlockSpec((None, seq, ML_V_W), lambda b: (b, 0, 1)),
                  pl.BlockSpec((None, seq, ML_V_W), lambda b: (b, 0, 2)),
                  pl.BlockSpec((None, seq, LANES), lambda b: (b, 0, 0)),
                  pl.BlockSpec((None, 3, 2 * ML_QK_W), lambda b: (layer, 0, 0)),
                  pl.BlockSpec((None, 1, LANES), lambda b: (layer, 0, 0)),
                  pl.BlockSpec((None, 1, ML_V_W), lambda b: (layer, 0, 0))],
        out_specs=pl.BlockSpec((None, seq, ML_V_W), lambda b: (b, 0, 0)),
        out_shape=jax.ShapeDtypeStruct((batch, seq, ML_V_W), jnp.bfloat16),
        scratch_shapes=[pltpu.VMEM((seq, ML_QK_W), jnp.bfloat16),
                        pltpu.VMEM((seq, ML_QK_W), jnp.bfloat16),
                        pltpu.VMEM((seq, LANES), jnp.float32),
                        pltpu.VMEM((seq, ML_V_W), jnp.float32),
                        pltpu.VMEM((seq, ML_V_W), jnp.float32),
                        pltpu.VMEM((2 * ML_HEADS, 2 * ML_DQK, 2 * ML_DV), jnp.float32)],
        compiler_params=_cparams(("parallel",), 56),
    )(z_ml, z_ml, z_ml, z_ml, z_g, conv_w, gate_b, norm_w)


def _alibi_slopes():
    n = ATT_NGROUP * ATT_HEADS
    s = 2.0 ** (-8.0 * np.arange(1, n + 1) / n)
    return s.reshape(ATT_NGROUP, ATT_HEADS).astype(np.float32)


ATT_SLABS = ATT_GW // LANES
ATT_STAGE_ROWS = 256


def _attn_group(q_ref, k_ref, v_ref, y_ref, stage_s, qs, ks, vs, oacc, lacc, osub, lsub, mix_s, bias_s, *,
                group, seq, first, last, prev_dil):
    dil = ATT_DIL[group]
    assert ATT_WINDOW[group] // (2 * dil) == ATT_NEIGH and (not last or dil == 1)
    assert prev_dil is None or (prev_dil % dil == 0 and ATT_QT % (prev_dil // dil) == 0)
    sub_len = seq // dil
    slopes = [float(s) for s in _alibi_slopes()[group] * np.float32(dil)]
    kw = min(2 * ATT_QT, sub_len)
    tiles_per_sub = sub_len // ATT_QT

    if dil == 1:
        q_src, k_src, v_src = q_ref, k_ref, v_ref
    else:
        for src, dst in ((q_ref, qs), (k_ref, ks), (v_ref, vs)):
            def stage(c, carry, src=src):
                rows = pl.ds(pl.multiple_of(c * ATT_STAGE_ROWS, ATT_STAGE_ROWS), ATT_STAGE_ROWS)
                x = src[rows, :].astype(jnp.float32)
                for sl in range(ATT_SLABS):
                    stage_s[sl, rows, :] = x[:, sl * LANES:(sl + 1) * LANES]
                return carry

            lax.fori_loop(0, seq // ATT_STAGE_ROWS, stage, 0)

            def gather_residue(r, carry, dst=dst):
                out_rows = pl.ds(pl.multiple_of(r * sub_len, sub_len), sub_len)
                for sl in range(ATT_SLABS):
                    x = stage_s[sl, pl.ds(r, sub_len, stride=dil), :]
                    dst[out_rows, sl * LANES:(sl + 1) * LANES] = x.astype(jnp.bfloat16)
                return carry

            lax.fori_loop(0, dil, gather_residue, 0)
        q_src, k_src, v_src = qs, ks, vs

    low_half = lax.broadcasted_iota(jnp.int32, (1, LANES), 1) < ATT_DH

    n_offsets = 1 if tiles_per_sub == 1 else 3
    rel0 = (lax.broadcasted_iota(jnp.int32, (ATT_QT, kw), 1) - lax.broadcasted_iota(jnp.int32, (ATT_QT, kw), 0))
    for oi in range(n_offsets):
        arel = jnp.abs(rel0 - oi * ATT_NEIGH).astype(jnp.float32)
        for h in range(ATT_HEADS):
            bias_s[oi * ATT_HEADS + h, :, 0:kw] = jnp.where(arel <= float(ATT_NEIGH), -(slopes[h] * arel), NEG)

    def tile(idx, carry):
        r = idx // tiles_per_sub
        q0 = (idx % tiles_per_sub) * ATT_QT
        k0 = jnp.clip(q0 - ATT_NEIGH, 0, sub_len - kw)
        base = r * sub_len
        bias_base = (q0 - k0) // ATT_NEIGH * ATT_HEADS
        q = q_src[pl.ds(pl.multiple_of(base + q0, ATT_QT), ATT_QT), :] * (ATT_DH ** -0.5)
        k = k_src[pl.ds(pl.multiple_of(base + k0, ATT_NEIGH), kw), :]
        v = v_src[pl.ds(pl.multiple_of(base + k0, ATT_NEIGH), kw), :]
        start = q0 * dil + r
        nat_rows = pl.ds(start, ATT_QT) if dil == 1 else pl.ds(start, ATT_QT, stride=dil)
        scores = []
        for sl in range(ATT_SLABS):
            cols = slice(sl * LANES, (sl + 1) * LANES)
            q2, k2 = q[:, cols], k[:, cols]
            for half in range(2):
                own = low_half if half == 0 else jnp.logical_not(low_half)
                scores.append(lax.dot_general(jnp.where(own, q2, jnp.zeros_like(q2)), k2,
                                              (((1,), (1,)), ((), ())), preferred_element_type=jnp.float32))
        probs = []
        for h in range(ATT_HEADS):
            s = scores[h] + bias_s[bias_base + h, :, 0:kw]
            m = jnp.max(s, axis=1, keepdims=True)
            p = jnp.exp(s - m)
            den = jnp.sum(p, axis=1, keepdims=True)
            probs.append((p.astype(jnp.bfloat16), den, m + jnp.log(den)))
        for sl in range(ATT_SLABS):
            cols = slice(sl * LANES, (sl + 1) * LANES)
            v2 = v[:, cols]
            (p0, den0, lse0), (p1, den1, lse1) = probs[2 * sl], probs[2 * sl + 1]
            o_sl = jnp.where(low_half, _bdot(p0, v2) / den0, _bdot(p1, v2) / den1)
            l_sl = jnp.where(low_half, lse0, lse1)
            if first:
                sub_rows = pl.ds(pl.multiple_of(base + q0, ATT_QT), ATT_QT)
                osub[sl, sub_rows, :] = o_sl
                lsub[sl, sub_rows, :] = l_sl
                continue
            if prev_dil is not None:
                ratio = prev_dil // dil
                seg = ATT_QT // ratio
                for c in range(ratio):
                    src_rows = pl.ds(pl.multiple_of((dil * c + r) * (seq // prev_dil) + q0 // ratio, seg), seg)
                    mix_s[0, pl.ds(c, seg, stride=ratio), :] = osub[sl, src_rows, :]
                    mix_s[1, pl.ds(c, seg, stride=ratio), :] = lsub[sl, src_rows, :]
                o_old, l_old = mix_s[0], mix_s[1]
            else:
                o_old, l_old = oacc[sl, nat_rows, :], lacc[sl, nat_rows, :]
            mx = jnp.maximum(l_old, l_sl)
            w_old, w_new = jnp.exp(l_old - mx), jnp.exp(l_sl - mx)
            tot = w_old + w_new
            o_sl = (w_old * o_old + w_new * o_sl) / tot
            l_sl = mx + jnp.log(tot)
            if last:
                y_ref[pl.ds(pl.multiple_of(start, ATT_QT), ATT_QT), cols] = o_sl.astype(y_ref.dtype)
            else:
                oacc[sl, nat_rows, :] = o_sl
                lacc[sl, nat_rows, :] = l_sl
        return carry

    lax.fori_loop(0, seq // ATT_QT, tile, 0)


def _attn_kernel(q_ref, k_ref, v_ref, y_ref, *scratch, seq):
    gi = pl.program_id(1)
    for step in range(ATT_NGROUP):
        @pl.when(gi == step)
        def _(step=step):
            group = ATT_NGROUP - 1 - step
            _attn_group(q_ref, k_ref, v_ref, y_ref, *scratch, group=group, seq=seq,
                        first=step == 0, last=step == ATT_NGROUP - 1,
                        prev_dil=ATT_DIL[group + 1] if step == 1 else None)


def _dilated_attention(z_at, batch, seq):
    def in_spec(which):
        return pl.BlockSpec((None, seq, ATT_GW),
                            lambda b, gi: (b, 0, which * ATT_NGROUP + ATT_NGROUP - 1 - gi))

    slab = pltpu.VMEM((ATT_SLABS, seq, LANES), jnp.float32)
    sub = pltpu.VMEM((seq, ATT_GW), jnp.bfloat16)
    return pl.pallas_call(
        functools.partial(_attn_kernel, seq=seq),
        grid=(batch, ATT_NGROUP),
        in_specs=[in_spec(0), in_spec(1), in_spec(2)],
        out_specs=pl.BlockSpec((None, seq, ATT_GW), lambda b, gi: (b, 0, 0)),
        out_shape=jax.ShapeDtypeStruct((batch, seq, ATT_GW), jnp.bfloat16),
        scratch_shapes=[slab, sub, sub, sub, slab, slab, slab, slab,
                        pltpu.VMEM((2, ATT_QT, LANES), jnp.float32),
                        pltpu.VMEM((3 * ATT_HEADS, ATT_QT, 2 * ATT_QT), jnp.float32)],
        compiler_params=_cparams(("parallel", "arbitrary"), 56),
    )(z_at, z_at, z_at)


def _gelu_tanh(x):
    return x * (0.5 * (1.0 + jnp.tanh(np.sqrt(2.0 / np.pi).astype(np.float32) * (x + 0.044715 * (x * x * x)))))


def _sg_kernel(u_ref, v_ref, g_ref, b_ref, ws_ref, bs_ref, y_ref):
    tm = u_ref.shape[0]
    dg = BRANCH_W // SG_GROUPS
    u = _gelu_tanh(u_ref[...])
    vn = _layer_norm(_gelu_tanh(v_ref[...]), g_ref[...], b_ref[...]).astype(jnp.bfloat16)
    bs = bs_ref[...]
    for c in range(tm // SG_CHUNK):
        rows = slice(c * SG_CHUNK, (c + 1) * SG_CHUNK)
        for g in range(SG_GROUPS):
            cols = slice(g * dg, (g + 1) * dg)
            mixed = _bdot(ws_ref[g], vn[rows, cols]) + bs[:, cols]
            y_ref[rows, cols] = (u[rows, cols] * mixed).astype(y_ref.dtype)


def _spatial_gating(z_sp, ln_g, ln_b, w_s, b_s_exp, layer, tm):
    t = z_sp.shape[0]
    return pl.pallas_call(
        _sg_kernel,
        grid=(t // tm,),
        in_specs=[pl.BlockSpec((tm, BRANCH_W), lambda i: (i, 0)),
                  pl.BlockSpec((tm, BRANCH_W), lambda i: (i, 1)),
                  pl.BlockSpec((None, 1, BRANCH_W), lambda i: (layer, 0, 0)),
                  pl.BlockSpec((None, 1, BRANCH_W), lambda i: (layer, 0, 0)),
                  pl.BlockSpec((None, SG_GROUPS, SG_CHUNK, SG_CHUNK), lambda i: (layer, 0, 0, 0)),
                  pl.BlockSpec((None, SG_CHUNK, BRANCH_W), lambda i: (layer, 0, 0))],
        out_specs=pl.BlockSpec((tm, BRANCH_W), lambda i: (i, 0)),
        out_shape=jax.ShapeDtypeStruct((t, BRANCH_W), jnp.bfloat16),
        compiler_params=_cparams(("parallel",), 32),
    )(z_sp, z_sp, ln_g, ln_b, w_s, b_s_exp)


def _pool_kernel(p_ref, w_ref, sc_ref, y_ref):
    s_len = p_ref.shape[0]
    row = lax.broadcasted_iota(jnp.int32, (s_len, 1), 0)
    sc = sc_ref[...]
    for g, win in enumerate(POOL_WINDOWS):
        cols = slice(g * POOL_DG, (g + 1) * POOL_DG)
        half = win // 2
        p = p_ref[:, cols]
        acc = jnp.zeros_like(p)
        for j in range(-half, half):
            if j == 0:
                acc = acc + p
            else:
                shifted = pltpu.roll(p, (-j) % s_len, 0)
                ok = (row + j >= 0) & (row + j < s_len)
                acc = acc + jnp.where(ok, shifted, 0.0)
        cnt = (jnp.minimum(row + half, s_len) - jnp.maximum(row - half, 0)).astype(jnp.float32)
        d = acc / cnt - p
        y = _bdot(d.astype(jnp.bfloat16), w_ref[g]) * sc[:, cols]
        y_ref[:, cols] = y.astype(y_ref.dtype)


def _multiscale_pool(z_sp, pool_w, pool_scale, layer, batch, seq):
    return pl.pallas_call(
        _pool_kernel,
        grid=(batch,),
        in_specs=[pl.BlockSpec((None, seq, BRANCH_W), lambda b: (b, 0, 2)),
                  pl.BlockSpec((None, len(POOL_WINDOWS), POOL_DG, POOL_DG), lambda b: (layer, 0, 0, 0)),
                  pl.BlockSpec((None, 1, BRANCH_W), lambda b: (layer, 0, 0))],
        out_specs=pl.BlockSpec((None, seq, BRANCH_W), lambda b: (b, 0, 0)),
        out_shape=jax.ShapeDtypeStruct((batch, seq, BRANCH_W), jnp.bfloat16),
        compiler_params=_cparams(("parallel",), 48),
    )(z_sp, pool_w, pool_scale)


def _gate_mix_kernel(xb_ref, y0, y1, y2, y3, wg0, wg1, wg2, wg3, bg0, bg1, bg2, bg3, wb_ref,
                     w1_ref, w3_ref, w2_ref, o_ref, w1_out, w3_out, w2_out):
    w1_out[...] = w1_ref[...].astype(w1_out.dtype)
    w3_out[...] = w3_ref[...].astype(w3_out.dtype)
    w2_out[...] = w2_ref[...].astype(w2_out.dtype)
    xb = xb_ref[...]
    merged = None
    for n, (y, wg, bg) in enumerate(((y0, wg0, bg0), (y1, wg1, bg1), (y2, wg2, bg2), (y3, wg3, bg3))):
        gate = _sigmoid(_bdot(xb, wg[...]) + bg[...])
        term = gate * _bdot(y[...], wb_ref[n])
        merged = term if merged is None else merged + term
    o_ref[...] = merged.astype(o_ref.dtype)


def _gate_mix(xb, ys, w_gate, b_gate, w_branch, w1, w3, w2, layer, tm, tn):
    t = xb.shape[0]
    nj = D_MODEL // tn
    row_spec = lambda w: pl.BlockSpec((tm, w), lambda i, j: (i, 0))
    wg_specs = [pl.BlockSpec((None, D_MODEL, tn), lambda i, j, n=n: (layer, 0, n * nj + j)) for n in range(N_BRANCH)]
    bg_specs = [pl.BlockSpec((None, 1, tn), lambda i, j, n=n: (layer, 0, n * nj + j)) for n in range(N_BRANCH)]
    steps = (t // tm) * nj
    split = max(steps // N_EXPERTS, 1)
    group = max(N_EXPERTS // steps, 1)
    assert split * N_EXPERTS == steps * group

    def cast_in(rows, cols):
        return pl.BlockSpec((None, group, rows // split, cols),
                            lambda i, j: (layer, (i * nj + j) // split, (i * nj + j) % split, 0))

    def cast_out(rows, cols):
        return pl.BlockSpec((group, rows // split, cols), lambda i, j: ((i * nj + j) // split, (i * nj + j) % split, 0))

    cast_shapes = ((D_MODEL, D_EXPERT), (D_MODEL, D_EXPERT), (D_EXPERT, D_MODEL))
    return pl.pallas_call(
        _gate_mix_kernel,
        grid=(t // tm, nj),
        in_specs=[row_spec(D_MODEL)] + [row_spec(BRANCH_W)] * N_BRANCH + wg_specs + bg_specs
                 + [pl.BlockSpec((None, N_BRANCH, BRANCH_W, tn), lambda i, j: (layer, 0, 0, j))]
                 + [cast_in(*s) for s in cast_shapes],
        out_specs=[pl.BlockSpec((tm, tn), lambda i, j: (i, j))] + [cast_out(*s) for s in cast_shapes],
        out_shape=[jax.ShapeDtypeStruct((t, D_MODEL), jnp.bfloat16)]
                  + [jax.ShapeDtypeStruct((N_EXPERTS,) + s, jnp.bfloat16) for s in cast_shapes],
        compiler_params=_cparams(("arbitrary", "arbitrary"), 56),
    )(xb, *ys, *([w_gate] * N_BRANCH), *([b_gate] * N_BRANCH), w_branch, w1, w3, w2)


def _out_proj_ln_kernel(m_ref, wo_ref, xf_ref, g_ref, b_ref, wr_ref, br_ref, of_ref, ob_ref,
                        id_ref, wt_ref, cnt_ref):
    out = _layer_norm(ALPHA * xf_ref[...] + _bdot(m_ref[...], wo_ref[...]), g_ref[...], b_ref[...])
    of_ref[...] = out
    ob_ref[...] = out.astype(ob_ref.dtype)
    _route(out, wr_ref, br_ref, id_ref, wt_ref, cnt_ref)


def _out_proj_ln(merged, w_out, xf, ln_g, ln_b, w_r, b_r, layer, tm):
    t = xf.shape[0]
    row_spec = pl.BlockSpec((tm, D_MODEL), lambda i: (i, 0))
    vec_spec = pl.BlockSpec((None, 1, D_MODEL), lambda i: (layer, 0, 0))
    lane_spec = pl.BlockSpec((tm, LANES), lambda i: (i, 0))
    return pl.pallas_call(
        _out_proj_ln_kernel,
        grid=(t // tm,),
        in_specs=[row_spec, pl.BlockSpec((None, D_MODEL, D_MODEL), lambda i: (layer, 0, 0)),
                  row_spec, vec_spec, vec_spec,
                  pl.BlockSpec((None, 2, D_MODEL, LANES), lambda i: (layer, 0, 0, 0)),
                  pl.BlockSpec((None, 1, LANES), lambda i: (layer, 0, 0))],
        out_specs=[row_spec, row_spec, lane_spec, lane_spec, pl.BlockSpec((None, 8, LANES), lambda i: (i, 0, 0))],
        out_shape=[jax.ShapeDtypeStruct((t, D_MODEL), jnp.float32),
                   jax.ShapeDtypeStruct((t, D_MODEL), jnp.bfloat16),
                   jax.ShapeDtypeStruct((t, LANES), jnp.int32),
                   jax.ShapeDtypeStruct((t, LANES), jnp.float32),
                   jax.ShapeDtypeStruct((t // tm, 8, LANES), jnp.int32)],
        compiler_params=_cparams(("parallel",), 52),
    )(merged, w_out, xf, ln_g, ln_b, w_r, b_r)


def _route(x, w_ref, b_ref, id_ref, wt_ref, cnt_ref):
    x_hi, x_lo = _split_hi_lo(x)
    w_hi, w_lo = w_ref[0], w_ref[1]
    logits = _bdot(x_hi, w_hi) + _bdot(x_lo, w_hi) + _bdot(x_hi, w_lo) + b_ref[...]
    lane = lax.broadcasted_iota(jnp.int32, logits.shape, 1)
    lane_f = lane.astype(jnp.float32)
    no_lane = float(LANES)

    def first_argmax(vals):
        mx = jnp.max(vals, axis=1, keepdims=True)
        idx = jnp.min(jnp.where(vals == mx, lane_f, no_lane), axis=1, keepdims=True)
        return mx, idx.astype(jnp.int32)

    is_group = lane < N_EGROUPS
    g_max, g_sel = first_argmax(jnp.where(is_group, logits, NEG))
    p_sel = 1.0 / jnp.sum(jnp.where(is_group, jnp.exp(logits - g_max), 0.0), axis=1, keepdims=True)

    lo = ROUTER_EOFF + g_sel * EXP_PER_GROUP
    cand = jnp.where((lane >= lo) & (lane < lo + EXP_PER_GROUP), logits, NEG)
    l1, i1 = first_argmax(cand)
    l2, i2 = first_argmax(jnp.where(lane == i1, NEG, cand))
    e = jnp.exp(l2 - l1)
    w1 = p_sel / (1.0 + e)
    w2 = p_sel * e / (1.0 + e)
    e1, e2 = i1 - ROUTER_EOFF, i2 - ROUTER_EOFF

    tm = logits.shape[0]
    hit1, hit2 = lane == e1, lane == e2
    hits = jnp.where(jnp.logical_or(hit1, hit2), 1.0, 0.0)
    earlier = (lax.broadcasted_iota(jnp.int32, (tm, tm), 1) < lax.broadcasted_iota(jnp.int32, (tm, tm), 0))
    before = _bdot(jnp.where(earlier, 1.0, 0.0).astype(jnp.bfloat16), hits.astype(jnp.bfloat16))
    rank1 = jnp.sum(jnp.where(hit1, before, 0.0), axis=1, keepdims=True).astype(jnp.int32)
    rank2 = jnp.sum(jnp.where(hit2, before, 0.0), axis=1, keepdims=True).astype(jnp.int32)
    counts = jnp.sum(hits, axis=0, keepdims=True).astype(jnp.int32)

    id_ref[...] = jnp.where(lane == 0, e1, jnp.where(lane == 1, e2,
                            jnp.where(lane == 2, rank1, jnp.where(lane == 3, rank2, 0))))
    wt_ref[...] = jnp.where(lane == 0, w1, jnp.where(lane == 1, w2, 0.0))
    cnt_ref[...] = jnp.broadcast_to(counts, cnt_ref.shape)


def _dispatch_plan(info, counts, blk, tm):
    t = info.shape[0]
    c = counts[:, 0, :N_EXPERTS]
    tile_off = jnp.cumsum(c, axis=0) - c
    total = jnp.sum(c, axis=0)
    padded = (total + blk - 1) // blk * blk
    pad_end = jnp.cumsum(padded)
    base = (pad_end - padded)[None, :] + tile_off
    base_tok = jnp.repeat(base, tm, axis=0)
    hit = info[:, :TOP_K, None] == jnp.arange(N_EXPERTS, dtype=jnp.int32)[None, None, :]
    dest = jnp.sum(jnp.where(hit, base_tok[:, None, :], 0), axis=2) + info[:, TOP_K:2 * TOP_K]
    nb = t * TOP_K // blk + N_EXPERTS
    blk_start = jnp.arange(nb, dtype=jnp.int32) * blk
    blk_e = jnp.minimum(jnp.sum((pad_end[None, :] <= blk_start[:, None]).astype(jnp.int32), axis=1),
                        N_EXPERTS - 1)
    n_used = (pad_end[-1] // blk).astype(jnp.int32).reshape(1)
    return dest.astype(jnp.int32), blk_e, n_used


def _dispatch_kernel(dest_ref, dest_prev_ref, x_ref, xs_in, xs_out, pk, sem):
    del xs_in
    i = pl.program_id(0)
    tm = x_ref.shape[0]
    half = D_MODEL // 2
    slot = lax.rem(i, 2)
    grp = ROW_DMA_GROUP

    def copy(idx_ref, s, r, j):
        return pltpu.make_async_copy(pk.at[s, pl.ds(r, 1), :], xs_out.at[pl.ds(idx_ref[0, j], 1), :], sem.at[s])

    def pack(t):
        rows = pl.ds(pl.multiple_of(t * grp, grp), grp)
        x = x_ref[rows, :].astype(jnp.float32)
        hi = lax.bitcast_convert_type(x[:, :half], jnp.uint32)
        lo = lax.bitcast_convert_type(x[:, half:], jnp.uint32)
        pk[slot, rows, :] = hi | (lo >> 16)

    def issue(t):
        for u in range(grp):
            for k in range(TOP_K):
                copy(dest_ref, slot, t * grp + u, k * tm + t * grp + u).start(priority=k)

    def group(t, carry):
        pack(t + 1)
        issue(t)
        return carry

    n_groups = tm // grp
    pack(0)
    lax.fori_loop(0, n_groups - 1, group, 0)
    issue(n_groups - 1)

    def wait_prev(j, carry):
        copy(dest_prev_ref, 1 - slot, lax.rem(j, tm), j).wait()
        return carry

    @pl.when(i > 0)
    def _():
        lax.fori_loop(0, TOP_K * tm, wait_prev, 0, unroll=8)

    def wait_own(j, carry):
        copy(dest_ref, slot, lax.rem(j, tm), j).wait()
        return carry

    @pl.when(i == pl.num_programs(0) - 1)
    def _():
        lax.fori_loop(0, TOP_K * tm, wait_own, 0, unroll=8)


def _dest_tiles(dest, tm):
    nt = dest.shape[0] // tm
    return dest.reshape(nt, tm, TOP_K).transpose(0, 2, 1).reshape(nt, 1, TOP_K * tm)


def _dispatch_rows(xb, dest_tiles, xs_prev):
    nt, _, two_tm = dest_tiles.shape
    tm = two_tm // TOP_K
    return pl.pallas_call(
        _dispatch_kernel,
        grid=(nt,),
        in_specs=[pl.BlockSpec((None, 1, two_tm), lambda i: (i, 0, 0), memory_space=pltpu.SMEM),
                  pl.BlockSpec((None, 1, two_tm), lambda i: (jnp.maximum(i - 1, 0), 0, 0),
                               memory_space=pltpu.SMEM),
                  pl.BlockSpec((tm, D_MODEL), lambda i: (i, 0)),
                  pl.BlockSpec(memory_space=pl.ANY)],
        out_specs=pl.BlockSpec(memory_space=pl.ANY),
        out_shape=jax.ShapeDtypeStruct(xs_prev.shape, xs_prev.dtype),
        scratch_shapes=[pltpu.VMEM((2, tm, D_MODEL // 2), jnp.uint32),
                        pltpu.SemaphoreType.DMA((2,))],
        input_output_aliases={3: 0},
        compiler_params=_cparams(("arbitrary",), 32),
    )(dest_tiles, dest_tiles, xb, xs_prev)


def _row_copy(src_hbm, idx, buf, slot, r, sem):
    return pltpu.make_async_copy(src_hbm.at[pl.ds(idx, 1), :], buf.at[slot, pl.ds(r, 1), :], sem.at[slot])


def _start_row_gather(src_hbm, idx_ref, buf, slot, sem, n_rows):
    def body(g, carry):
        for u in range(ROW_DMA_UNROLL):
            r = g * ROW_DMA_UNROLL + u
            _row_copy(src_hbm, idx_ref[0, r], buf, slot, r, sem).start(priority=u % 2)
        return carry
    lax.fori_loop(0, n_rows // ROW_DMA_UNROLL, body, 0)


def _wait_row_gather(src_hbm, idx_ref, buf, slot, sem, n_rows):
    def body(r, carry):
        _row_copy(src_hbm, idx_ref[0, r], buf, slot, r, sem).wait()
        return carry
    lax.fori_loop(0, n_rows, body, 0, unroll=8)


def _expert_kernel(blk_e_ref, n_used_ref, xs_ref, w1_ref, w3_ref, w2_ref, y_ref):
    del blk_e_ref
    i = pl.program_id(0)
    n_used = n_used_ref[0]

    @pl.when(i < n_used)
    def _():
        packed = xs_ref[...]
        x_a = lax.bitcast_convert_type(packed & jnp.uint32(0xFFFF0000), jnp.float32)
        x_b = lax.bitcast_convert_type(packed << 16, jnp.float32)
        xb = jnp.concatenate([x_a, x_b], axis=1).astype(jnp.bfloat16)
        a = _bdot(xb, w1_ref[...])
        h = (a * _sigmoid(a)) * _bdot(xb, w3_ref[...])
        y_ref[...] = _bdot(h.astype(jnp.bfloat16), w2_ref[...])

    @pl.when(i >= n_used)
    def _():
        y_ref[...] = jnp.zeros(y_ref.shape, y_ref.dtype)


def _experts(xs, blk_e, n_used, w1, w3, w2, blk):
    nb = xs.shape[0] // blk

    def used(i, nu):
        return jnp.minimum(i, nu[0] - 1)

    def w_index(i, be, nu):
        return (be[used(i, nu)], 0, 0)

    grid_spec = pltpu.PrefetchScalarGridSpec(
        num_scalar_prefetch=2,
        grid=(nb,),
        in_specs=[pl.BlockSpec((blk, D_MODEL // 2), lambda i, be, nu: (used(i, nu), 0)),
                  pl.BlockSpec((None, D_MODEL, D_EXPERT), w_index),
                  pl.BlockSpec((None, D_MODEL, D_EXPERT), w_index),
                  pl.BlockSpec((None, D_EXPERT, D_MODEL), w_index)],
        out_specs=pl.BlockSpec((blk, D_MODEL), lambda i, be, nu: (i, 0)))
    return pl.pallas_call(
        _expert_kernel,
        grid_spec=grid_spec,
        out_shape=jax.ShapeDtypeStruct((nb * blk, D_MODEL), jnp.float32),
        compiler_params=_cparams(("arbitrary",), 40),
    )(blk_e, n_used, xs, w1, w3, w2)


def _combine_kernel(pos_ref, pos_next_ref, y_hbm, xf_ref, wt_ref, g_ref, b_ref, of_ref, ob_ref, ybuf, sem):
    i = pl.program_id(0)
    tm = xf_ref.shape[0]
    slot = lax.rem(i, 2)

    @pl.when(i == 0)
    def _():
        _start_row_gather(y_hbm, pos_ref, ybuf, 0, sem, TOP_K * tm)

    _wait_row_gather(y_hbm, pos_ref, ybuf, slot, sem, TOP_K * tm)
    gain, shift = g_ref[...], b_ref[...]
    grp = ROW_DMA_GROUP

    def make_group(issue_next):
        def group(t, carry):
            r0 = pl.multiple_of(t * grp, grp)
            rows = pl.ds(r0, grp)
            wt = wt_ref[rows, :]
            moe = wt[:, 0:1] * ybuf[slot, rows, :] + wt[:, 1:2] * ybuf[slot, pl.ds(tm + r0, grp), :]
            out = _layer_norm(ALPHA * xf_ref[rows, :] + moe, gain, shift)
            of_ref[rows, :] = out
            ob_ref[rows, :] = out.astype(ob_ref.dtype)
            if issue_next:
                for u in range(TOP_K * grp):
                    j = t * (TOP_K * grp) + u
                    _row_copy(y_hbm, pos_next_ref[0, j], ybuf, 1 - slot, j, sem).start(priority=u % 2)
            return carry
        return group

    has_next = i + 1 < pl.num_programs(0)

    @pl.when(has_next)
    def _():
        lax.fori_loop(0, tm // grp, make_group(True), 0)

    @pl.when(jnp.logical_not(has_next))
    def _():
        lax.fori_loop(0, tm // grp, make_group(False), 0)


def _combine(y_sorted, pos_tiles, xf, wts, ln_g, ln_b, layer):
    t = xf.shape[0]
    nt = pos_tiles.shape[0]
    tm = t // nt
    row_spec = pl.BlockSpec((tm, D_MODEL), lambda i: (i, 0))
    vec_spec = pl.BlockSpec((None, 1, D_MODEL), lambda i: (layer, 0, 0))
    return pl.pallas_call(
        _combine_kernel,
        grid=(nt,),
        in_specs=[pl.BlockSpec((None, 1, TOP_K * tm), lambda i: (i, 0, 0), memory_space=pltpu.SMEM),
                  pl.BlockSpec((None, 1, TOP_K * tm), lambda i: (jnp.minimum(i + 1, nt - 1), 0, 0),
                               memory_space=pltpu.SMEM),
                  pl.BlockSpec(memory_space=pl.ANY),
                  row_spec,
                  pl.BlockSpec((tm, LANES), lambda i: (i, 0)),
                  vec_spec, vec_spec],
        out_specs=[row_spec, row_spec],
        out_shape=[jax.ShapeDtypeStruct((t, D_MODEL), jnp.float32),
                   jax.ShapeDtypeStruct((t, D_MODEL), jnp.bfloat16)],
        scratch_shapes=[pltpu.VMEM((2, TOP_K * tm, D_MODEL), jnp.float32),
                        pltpu.SemaphoreType.DMA((2,))],
        compiler_params=_cparams(("arbitrary",), 48),
    )(pos_tiles, pos_tiles, y_sorted, xf, wts, ln_g, ln_b)


MOE_BLOCK_ROWS = 256
PROJ_TM, PROJ_TN = 2048, 512
MIX_TM, MIX_TN = 1024, 256
OUT_TM = 512
SG_TM = 512
ROW_DMA_TM = 256


def _mixer_layer(xf, xb, p, layer, batch, seq):
    t = batch * seq
    z_ml = _matmul(xb, p["w_ml"], layer, jnp.float32, PROJ_TM, PROJ_TN)
    z_g = _matmul(xb, p["w_mg"], layer, jnp.float32, PROJ_TM, LANES)
    z_at = _matmul(xb, p["w_at"], layer, jnp.bfloat16, PROJ_TM, PROJ_TN)
    z_sp = _matmul(xb, p["w_sp"], layer, jnp.float32, PROJ_TM, PROJ_TN)
    y_ml = _mlstm(z_ml.reshape(batch, seq, ML_W), z_g.reshape(batch, seq, LANES),
                  p["ml_conv_w"], p["ml_gate_b"], p["ml_norm_w"], layer, batch, seq).reshape(t, BRANCH_W)
    y_at = _dilated_attention(z_at.reshape(batch, seq, 3 * ATT_W), batch, seq).reshape(t, BRANCH_W)
    y_sg = _spatial_gating(z_sp, p["sg_ln_g"], p["sg_ln_b"], p["sg_w"], p["sg_b"], layer, SG_TM)
    y_pl = _multiscale_pool(z_sp.reshape(batch, seq, 3 * BRANCH_W), p["pool_w"], p["pool_scale"],
                            layer, batch, seq).reshape(t, BRANCH_W)
    merged, *expert_w = _gate_mix(xb, (y_ml, y_at, y_sg, y_pl), p["w_gate"], p["b_gate"], p["w_branch"],
                                  p["w_exp_gate"], p["w_exp_up"], p["w_exp_down"], layer, MIX_TM, MIX_TN)
    xf, xb, *routing = _out_proj_ln(merged, p["w_out"], xf, p["ln1_g"], p["ln1_b"], p["w_router"],
                                    p["b_router"], layer, OUT_TM)
    return xf, xb, routing, expert_w


def _moe_layer(xf, xb, routing, expert_w, xs_prev, p, layer):
    info, wts, counts = routing
    dest, blk_e, n_used = _dispatch_plan(info, counts, MOE_BLOCK_ROWS, OUT_TM)
    dest_tiles = _dest_tiles(dest, ROW_DMA_TM)
    xs = _dispatch_rows(xb, dest_tiles, xs_prev)
    y_sorted = _experts(xs, blk_e, n_used, *expert_w, MOE_BLOCK_ROWS)
    xf, xb = _combine(y_sorted, dest_tiles, xf, wts, p["ln2_g"], p["ln2_b"], layer)
    return xf, xb, xs


def _prepare_params(w_in, ml_conv_w, ml_gate_b, ml_norm_w, sg_ln_g, sg_ln_b, sg_w, sg_b, pool_w, pool_scale,
                    w_gate, b_gate, w_branch, w_out, ln1_g, ln1_b, w_router_group, b_router_group,
                    w_router_expert, b_router_expert, w_exp_gate, w_exp_up, w_exp_down, ln2_g, ln2_b):
    bf = jnp.bfloat16
    depth = w_in.shape[0]
    c_ml, c_g = ML_W, ML_W + ML_GATES
    c_at = c_g + 3 * ATT_W
    row = lambda a: a.reshape(depth, 1, -1)
    lane_pad = lambda a: jnp.pad(a, [(0, 0)] * (a.ndim - 1) + [(0, LANES - a.shape[-1])])
    w_r = jnp.zeros((depth, D_MODEL, LANES), jnp.float32)
    w_r = w_r.at[:, :, :N_EGROUPS].set(w_router_group).at[:, :, ROUTER_EOFF:ROUTER_EOFF + N_EXPERTS].set(w_router_expert)
    b_r = jnp.zeros((depth, LANES), jnp.float32)
    b_r = b_r.at[:, :N_EGROUPS].set(b_router_group).at[:, ROUTER_EOFF:ROUTER_EOFF + N_EXPERTS].set(b_router_expert)
    w_r_hi = w_r.astype(bf)
    w_r_lo = (w_r - w_r_hi.astype(jnp.float32)).astype(bf)
    sg_b_exp = jnp.repeat(jnp.swapaxes(sg_b, 1, 2), BRANCH_W // SG_GROUPS, axis=2)
    return dict(
        w_ml=w_in[:, :, :c_ml].astype(bf),
        w_mg=lane_pad(w_in[:, :, c_ml:c_g]).astype(bf),
        w_at=w_in[:, :, c_g:c_at].astype(bf),
        w_sp=w_in[:, :, c_at:].astype(bf),
        ml_conv_w=ml_conv_w,
        ml_gate_b=lane_pad(ml_gate_b.reshape(depth, 1, ML_GATES)),
        ml_norm_w=row(ml_norm_w),
        sg_ln_g=row(sg_ln_g), sg_ln_b=row(sg_ln_b), sg_w=sg_w.astype(bf), sg_b=sg_b_exp,
        pool_w=pool_w.astype(bf), pool_scale=row(pool_scale),
        w_gate=w_gate.astype(bf), b_gate=row(b_gate), w_branch=w_branch.astype(bf), w_out=w_out.astype(bf),
        ln1_g=row(ln1_g), ln1_b=row(ln1_b),
        w_router=jnp.stack([w_r_hi, w_r_lo], axis=1), b_router=row(b_r),
        w_exp_gate=w_exp_gate, w_exp_up=w_exp_up, w_exp_down=w_exp_down,
        ln2_g=row(ln2_g), ln2_b=row(ln2_b))


def kernel(x, w_in, ml_conv_w, ml_gate_b, ml_norm_w, sg_ln_g, sg_ln_b, sg_w, sg_b, pool_w, pool_scale, w_gate,
           b_gate, w_branch, w_out, ln1_g, ln1_b, w_router_group, b_router_group, w_router_expert,
           b_router_expert, w_exp_gate, w_exp_up, w_exp_down, ln2_g, ln2_b):
    batch, seq, d = x.shape
    assert d == D_MODEL and seq % (ATT_DIL[-1] * ATT_QT) == 0
    p = _prepare_params(w_in, ml_conv_w, ml_gate_b, ml_norm_w, sg_ln_g, sg_ln_b, sg_w, sg_b, pool_w, pool_scale,
                        w_gate, b_gate, w_branch, w_out, ln1_g, ln1_b, w_router_group, b_router_group,
                        w_router_expert, b_router_expert, w_exp_gate, w_exp_up, w_exp_down, ln2_g, ln2_b)
    t = batch * seq
    xf = x.reshape(t, d)
    xb = xf.astype(jnp.bfloat16)
    n_slots = (t * TOP_K // MOE_BLOCK_ROWS + N_EXPERTS) * MOE_BLOCK_ROWS
    xs = jnp.zeros((n_slots, d // 2), jnp.uint32)
    for layer in range(w_in.shape[0]):
        xf, xb, routing, expert_w = _mixer_layer(xf, xb, p, layer, batch, seq)
        xf, xb, xs = _moe_layer(xf, xb, routing, expert_w, xs, p, layer)
    return xf.reshape(batch, seq, d)
```
